```python
import jax, jax.numpy as jnp
from jax import lax
import numpy as np

D_MODEL = 1024
BATCH = 8
SEQ = 2048
DEPTH = 4
DEC_BATCH = 128
DEC_SEQ = 4
PAST_LEN = 2048
PAGE_SIZE = 128

HEAD_DIM = 64
N_HEADS = D_MODEL // HEAD_DIM
H_A = N_HEADS // 2
H_B = N_HEADS - H_A
W_A = H_A * HEAD_DIM
W_B = H_B * HEAD_DIM
CHUNK = 128
Q_BLOCK = 128
D_FF = -(-8 * D_MODEL // (3 * 256)) * 256
D_IN = 2 * W_A + 3 * W_B + H_B
EPS = 1e-6
FGATE_BIAS_LO = 2.0
FGATE_BIAS_HI = 8.0

kernel_name = "hymba_sgu_fox_adaln_step"


def rms_norm(x, g):
    xf = x.astype(jnp.float32)
    y = xf * lax.rsqrt(jnp.mean(xf * xf, axis=-1, keepdims=True) + EPS)
    return (y * g.astype(jnp.float32)).astype(x.dtype)


def adaln_mod(c, w_ada, b_ada):
    m = jnp.einsum('bd,de->be', jax.nn.silu(c), w_ada) + b_ada
    return jnp.split(m[:, None, :], 6, axis=-1)


def modulate(h, shift, scale):
    return h * (1 + scale) + shift


def mixer_inputs(h, w_in, b_f, g_sgu):
    z = jnp.einsum('btd,de->bte', h, w_in)
    lead = z.shape[:2]
    o = 2 * W_A
    u = z[..., :W_A]
    va = z[..., W_A:o]
    q = z[..., o:o + W_B]
    k = z[..., o + W_B:o + 2 * W_B]
    v = z[..., o + 2 * W_B:o + 3 * W_B]
    fl = z[..., o + 3 * W_B:]
    hs = lambda t, n: t.reshape(lead + (n, HEAD_DIM))
    u = hs(jax.nn.gelu(u), H_A)
    va = rms_norm(hs(jax.nn.gelu(va), H_A), g_sgu)
    logf = jax.nn.log_sigmoid(fl.astype(jnp.float32) + b_f.astype(jnp.float32))
    return u, va, hs(q, H_B), hs(k, H_B), hs(v, H_B), logf


def sgu_mix(u, v, w_s, b_s):
    L = u.shape[2]
    w = w_s[:, :L, :L] * jnp.tril(jnp.ones((L, L), w_s.dtype))
    mixed = jnp.einsum('hij,bnjhd->bnihd', w, v) + b_s[:, :L].T[None, None, :, :, None]
    return u * mixed


def fox_prompt(q, k, v, logf):
    B, S = q.shape[:2]
    Ft = jnp.transpose(jnp.cumsum(logf, axis=1), (0, 2, 1))
    scale = HEAD_DIM ** -0.5
    kpos = jnp.arange(S)

    def block(start):
        qb = lax.dynamic_slice_in_dim(q, start, Q_BLOCK, axis=1)
        Fb = lax.dynamic_slice_in_dim(Ft, start, Q_BLOCK, axis=2)
        s = jnp.einsum('bqhd,bkhd->bhqk', qb, k).astype(jnp.float32) * scale
        s = s + Fb[..., None] - Ft[:, :, None, :]
        qpos = start + jnp.arange(Q_BLOCK)
        s = jnp.where(kpos[None, :] <= qpos[:, None], s, -jnp.inf)
        p = jax.nn.softmax(s, axis=-1).astype(v.dtype)
        return jnp.einsum('bhqk,bkhd->bqhd', p, v)

    out = lax.map(block, jnp.arange(S // Q_BLOCK) * Q_BLOCK)
    return jnp.moveaxis(out, 0, 1).reshape(B, S, H_B, HEAD_DIM)


def fox_sample(q, k, v, logf, k_past, v_past, f_past):
    T = q.shape[1]
    P = k_past.shape[1]
    scale = HEAD_DIM ** -0.5
    Gt = jnp.transpose(jnp.cumsum(logf, axis=1), (0, 2, 1))
    Rt = jnp.transpose(lax.cumsum(f_past, axis=1, reverse=True) - f_past, (0, 2, 1))
    s_past = (jnp.einsum('bqhd,bkhd->bhqk', q, k_past).astype(jnp.float32) * scale
              + Gt[..., None] + Rt[:, :, None, :])
    s_new = (jnp.einsum('bqhd,bkhd->bhqk', q, k).astype(jnp.float32) * scale
             + Gt[..., None] - Gt[:, :, None, :])
    s_new = jnp.where(jnp.tril(jnp.ones((T, T), bool)), s_new, -jnp.inf)
    p = jax.nn.softmax(jnp.concatenate([s_past, s_new], axis=-1), axis=-1).astype(v.dtype)
    return (jnp.einsum('bhqk,bkhd->bqhd', p[..., :P], v_past)
            + jnp.einsum('bhqk,bkhd->bqhd', p[..., P:], v))


def mix_out(a, o, w_out):
    lead = a.shape[:2]
    cat = jnp.concatenate([a.reshape(lead + (W_A,)), o.reshape(lead + (W_B,))], axis=-1)
    return jnp.einsum('bte,ed->btd', cat, w_out)


def residual_tail(x, y_mix, mod, g_post_mix, g_pre_ffn, g_post_ffn, w_gu, w_down):
    x = x + mod[2] * rms_norm(y_mix, g_post_mix)
    h = modulate(rms_norm(x, g_pre_ffn), mod[3], mod[4])
    gu = jnp.einsum('btd,df->btf', h, w_gu)
    f = jnp.einsum('btf,fd->btd', jax.nn.silu(gu[..., :D_FF]) * gu[..., D_FF:], w_down)
    return x + mod[5] * rms_norm(f, g_post_ffn)


def setup_inputs(seed: int = 0) -> dict:
    key = jax.random.key(seed)
    ks = jax.random.split(key, 24)
    n_pages = PAST_LEN // PAGE_SIZE
    n_used = DEC_BATCH * n_pages
    n_phys = (n_used * 5) // 4
    nrm = lambda k, shp, s: jax.random.normal(k, shp, jnp.float32) * s
    page_table = jax.random.permutation(ks[0], n_phys)[:n_used].reshape(DEC_BATCH, n_pages).astype(jnp.int32)
    head_bias = jnp.linspace(FGATE_BIAS_LO, FGATE_BIAS_HI, H_B, dtype=jnp.float32)
    logf_pre = head_bias + nrm(ks[1], (n_phys, DEPTH, PAGE_SIZE, H_B), 0.5)
    return {
        "x_prompt": nrm(ks[2], (BATCH, SEQ, D_MODEL), 1.0),
        "x_sample": nrm(ks[3], (DEC_BATCH, DEC_SEQ, D_MODEL), 1.0),
        "c_prompt": nrm(ks[4], (BATCH, D_MODEL), 1.0),
        "c_sample": nrm(ks[5], (DEC_BATCH, D_MODEL), 1.0),
        "cache_k": nrm(ks[6], (n_phys, DEPTH, PAGE_SIZE, H_B, HEAD_DIM), 1.0),
        "cache_v": nrm(ks[7], (n_phys, DEPTH, PAGE_SIZE, H_B, HEAD_DIM), 1.0),
        "cache_logf": jax.nn.log_sigmoid(logf_pre),
        "page_table": page_table,
        "w_ada": nrm(ks[9], (DEPTH, D_MODEL, 6 * D_MODEL), 0.5 * D_MODEL ** -0.5),
        "b_ada": nrm(ks[10], (DEPTH, 6 * D_MODEL), 0.02),
        "g_pre_mix": 1.0 + nrm(ks[11], (DEPTH, D_MODEL), 0.02),
        "g_post_mix": 1.0 + nrm(ks[12], (DEPTH, D_MODEL), 0.02),
        "g_pre_ffn": 1.0 + nrm(ks[13], (DEPTH, D_MODEL), 0.02),
        "g_post_ffn": 1.0 + nrm(ks[14], (DEPTH, D_MODEL), 0.02),
        "w_in": nrm(ks[15], (DEPTH, D_MODEL, D_IN), D_MODEL ** -0.5),
        "b_f": jax.random.uniform(ks[16], (DEPTH, H_B), jnp.float32, FGATE_BIAS_LO, FGATE_BIAS_HI),
        "g_sgu": 1.0 + nrm(ks[17], (DEPTH, H_A, HEAD_DIM), 0.02),
        "w_s": nrm(ks[18], (DEPTH, H_A, CHUNK, CHUNK), CHUNK ** -0.5),
        "b_s": 1.0 + nrm(ks[19], (DEPTH, H_A, CHUNK), 0.1),
        "w_out": nrm(ks[20], (DEPTH, D_MODEL, D_MODEL), D_MODEL ** -0.5),
        "w_gu": nrm(ks[21], (DEPTH, D_MODEL, 2 * D_FF), D_MODEL ** -0.5),
        "w_down": nrm(ks[22], (DEPTH, D_FF, D_MODEL), D_FF ** -0.5),
    }


def reference(x_prompt, x_sample, c_prompt, c_sample, cache_k, cache_v, cache_logf, page_table,
              w_ada, b_ada, g_pre_mix, g_post_mix, g_pre_ffn, g_post_ffn, w_in, b_f, g_sgu,
              w_s, b_s, w_out, w_gu, w_down):
    B, S, _ = x_prompt.shape
    DB, T, _ = x_sample.shape
    past = page_table.shape[1] * cache_k.shape[2]
    xp, xs = x_prompt, x_sample
    pk, pv, pf, sk, sv, sf, sa = [], [], [], [], [], [], []
    for l in range(DEPTH):
        mod_p = adaln_mod(c_prompt, w_ada[l], b_ada[l])
        mod_s = adaln_mod(c_sample, w_ada[l], b_ada[l])

        h = modulate(rms_norm(xp, g_pre_mix[l]), mod_p[0], mod_p[1])
        u, va, q, k, v, logf = mixer_inputs(h, w_in[l], b_f[l], g_sgu[l])
        chunked = (B, S // CHUNK, CHUNK, H_A, HEAD_DIM)
        a = sgu_mix(u.reshape(chunked), va.reshape(chunked), w_s[l], b_s[l])
        o = fox_prompt(q, k, v, logf)
        xp = residual_tail(xp, mix_out(a.reshape(B, S, H_A, HEAD_DIM), o, w_out[l]), mod_p,
                           g_post_mix[l], g_pre_ffn[l], g_post_ffn[l], w_gu[l], w_down[l])
        pk.append(k); pv.append(v); pf.append(logf)

        h = modulate(rms_norm(xs, g_pre_mix[l]), mod_s[0], mod_s[1])
        u, va, q, k, v, logf = mixer_inputs(h, w_in[l], b_f[l], g_sgu[l])
        a = sgu_mix(u[:, None], va[:, None], w_s[l], b_s[l])[:, 0]
        k_past = cache_k[page_table, l].reshape(DB, past, H_B, HEAD_DIM)
        v_past = cache_v[page_table, l].reshape(DB, past, H_B, HEAD_DIM)
        f_past = cache_logf[page_table, l].reshape(DB, past, H_B).astype(jnp.float32)
        o = fox_sample(q, k, v, logf, k_past, v_past, f_past)
        xs = residual_tail(xs, mix_out(a, o, w_out[l]), mod_s,
                           g_post_mix[l], g_pre_ffn[l], g_post_ffn[l], w_gu[l], w_down[l])
        sk.append(k); sv.append(v); sf.append(logf); sa.append(va)

    npp = S // PAGE_SIZE
    paged = lambda t: jnp.moveaxis(t.reshape((B, DEPTH, npp, PAGE_SIZE) + t.shape[3:]), 2, 1)
    prompt_k = paged(jnp.stack(pk, axis=1))
    prompt_v = paged(jnp.stack(pv, axis=1))
    prompt_logf = paged(jnp.stack(pf, axis=1))
    sample_k = jnp.stack(sk, axis=1)
    sample_v = jnp.stack(sv, axis=1)
    sample_logf = jnp.stack(sf, axis=1)
    sample_sgu_v = jnp.stack(sa, axis=1)
    return (xp, xs, prompt_k, prompt_v, prompt_logf, sample_k, sample_v, sample_logf, sample_sgu_v)
```

```python
import functools

import jax
import jax.numpy as jnp
from jax import lax
from jax.experimental import pallas as pl
from jax.experimental.pallas import tpu as pltpu

F32 = jnp.float32
BF16 = jnp.bfloat16

HEAD_DIM = 64
CHUNK = 128
EPS = 1e-6
LANES = 128
SUBLANES = 8
NEG = -1e30
VMEM_LIMIT = 56 * 1024 * 1024

_NT = (((1,), (1,)), ((), ()))


def _params(sem, vmem=VMEM_LIMIT):
    return pltpu.CompilerParams(dimension_semantics=sem, vmem_limit_bytes=vmem)


def _rms(x, g):
    return x * lax.rsqrt(jnp.mean(x * x, axis=-1, keepdims=True) + EPS) * g


def _dot(a, b):
    return jnp.dot(a, b, preferred_element_type=F32)


def _mod(x, n):
    return jnp.bitwise_and(x, n - 1) if n & (n - 1) == 0 else x % n


def _div(x, n):
    return lax.shift_right_logical(x, n.bit_length() - 1) if n & (n - 1) == 0 else x // n


def _split3(x):
    x1 = x.astype(BF16)
    r1 = x - x1.astype(F32)
    x2 = r1.astype(BF16)
    x3 = (r1 - x2.astype(F32)).astype(BF16)
    return x1, x2, x3


def _ada_kernel(c_ref, w_ref, b_ref, o_ref):
    c = c_ref[...]
    s = (c * jax.nn.sigmoid(c)).astype(BF16)
    o_ref[...] = _dot(s, w_ref[...].astype(BF16)) + b_ref[...]


def _ada(c_all, w_ada, b_ada, tn=1024):
    depth, d, e = w_ada.shape
    n = c_all.shape[0]
    return pl.pallas_call(
        _ada_kernel,
        grid=(depth, e // tn),
        in_specs=[
            pl.BlockSpec((n, d), lambda l, j: (0, 0)),
            pl.BlockSpec((None, d, tn), lambda l, j: (l, 0, j)),
            pl.BlockSpec((None, 1, tn), lambda l, j: (l, 0, j)),
        ],
        out_specs=pl.BlockSpec((None, n, tn), lambda l, j: (l, 0, j)),
        out_shape=jax.ShapeDtypeStruct((depth, n, e), F32),
        compiler_params=_params(("arbitrary", "arbitrary")),
        name="ada",
    )(c_all, w_ada, b_ada.reshape(depth, 1, e))


def _mix_in_body(x_ref, shift_ref, scale_ref, g_ref, w_ref, wf_ref, bf_ref, gsgu_ref, ones_ref):
    wa = ones_ref.shape[0]
    x = x_ref[...]
    h = _rms(x, g_ref[...]) * (1.0 + scale_ref[...]) + shift_ref[...]
    hb = h.astype(BF16)
    z = _dot(hb, w_ref[...])
    u = jax.nn.gelu(z[:, :wa])
    va = jax.nn.gelu(z[:, wa:2 * wa])
    gs = _dot((va * va).astype(BF16), ones_ref[...])
    va = va * lax.rsqrt(gs * (1.0 / HEAD_DIM) + EPS) * gsgu_ref[...]
    o = 2 * wa
    wb = (z.shape[1] - o) // 3
    q = z[:, o:o + wb] * (HEAD_DIM ** -0.5)
    k = z[:, o + wb:o + 2 * wb]
    v = z[:, o + 2 * wb:o + 3 * wb]
    nh = bf_ref.shape[0]
    fl = lax.dot_general(wf_ref[...], hb, _NT, preferred_element_type=F32)[:nh] + bf_ref[...]
    logf_t = jnp.minimum(fl, 0.0) - jnp.log1p(jnp.exp(-jnp.abs(fl)))
    return u, va, q, k, v, logf_t


def _mix_in_prompt_kernel(x_ref, shift_ref, scale_ref, g_ref, w_ref, wf_ref, bf_ref, gsgu_ref,
                          ones_ref, ws_ref, bias_ref,
                          a_ref, q_ref, k_ref, v_ref, lf_ref, fc_ref, carry_ref):
    tm = x_ref.shape[0]
    u, va, q, k, v, logf_t = _mix_in_body(x_ref, shift_ref, scale_ref, g_ref, w_ref, wf_ref,
                                          bf_ref, gsgu_ref, ones_ref)
    q_ref[...] = q.astype(BF16)
    k_ref[...] = k
    v_ref[...] = v
    lf_ref[...] = logf_t

    @pl.when(pl.program_id(1) == 0)
    def _():
        carry_ref[...] = jnp.zeros_like(carry_ref)

    r = lax.broadcasted_iota(jnp.int32, (tm, tm), 0)
    c = lax.broadcasted_iota(jnp.int32, (tm, tm), 1)
    tri = jnp.where(r <= c, 1.0, 0.0).astype(BF16)
    l1, l2, l3 = _split3(logf_t)
    fc = _dot(l1, tri) + _dot(l2, tri) + _dot(l3, tri) + carry_ref[:, :1]
    fc_ref[...] = fc
    carry_ref[...] = jnp.broadcast_to(fc[:, tm - 1:tm], carry_ref.shape)

    n_heads = ws_ref.shape[0]
    rr = lax.broadcasted_iota(jnp.int32, (CHUNK, CHUNK), 0)
    cc = lax.broadcasted_iota(jnp.int32, (CHUNK, CHUNK), 1)
    low = cc <= rr
    first = lax.broadcasted_iota(jnp.int32, (CHUNK, LANES), 1) < HEAD_DIM
    wts = [jnp.where(low, ws_ref[hh], 0.0).astype(BF16) for hh in range(n_heads)]
    for ci in range(tm // CHUNK):
        rows = slice(ci * CHUNK, (ci + 1) * CHUNK)
        for p in range(n_heads // 2):
            cols = slice(p * LANES, (p + 1) * LANES)
            vp = va[rows, cols].astype(BF16)
            mixed = jnp.where(first, _dot(wts[2 * p], vp), _dot(wts[2 * p + 1], vp))
            a_ref[rows, cols] = (u[rows, cols] * (mixed + bias_ref[:, cols])).astype(BF16)


def _mix_in_sample_kernel(x_ref, shift_ref, scale_ref, g_ref, w_ref, wf_ref, bf_ref, gsgu_ref,
                          ones_ref, u_ref, va_ref, q_ref, k_ref, v_ref, lf_ref):
    u, va, q, k, v, logf_t = _mix_in_body(x_ref, shift_ref, scale_ref, g_ref, w_ref, wf_ref,
                                          bf_ref, gsgu_ref, ones_ref)
    u_ref[...] = u
    va_ref[...] = va
    q_ref[...] = q.astype(BF16)
    k_ref[...] = k
    v_ref[...] = v
    lf_ref[...] = logf_t


def _const_spec(shape):
    nd = len(shape)
    return pl.BlockSpec(shape, lambda *_: (0,) * nd)


def _mix_in_prompt(x, mods, g_pre, w_main, wf_t, b_f, g_sgu, ones_bd, w_s, bias_exp, tm=256):
    b, s, d = x.shape
    wa = ones_bd.shape[0]
    wb = (w_main.shape[1] - 2 * wa) // 3
    nh = b_f.shape[0]
    row = lambda width: pl.BlockSpec((None, tm, width), lambda bi, i: (bi, i, 0))
    mod = lambda j: pl.BlockSpec((None, 1, d), lambda bi, i: (bi, 0, j))
    lane_major = pl.BlockSpec((None, nh, tm), lambda bi, i: (bi, 0, i))
    return pl.pallas_call(
        _mix_in_prompt_kernel,
        grid=(b, s // tm),
        in_specs=[row(d), mod(0), mod(1), _const_spec((1, d)), _const_spec(w_main.shape),
                  _const_spec(wf_t.shape), _const_spec((nh, 1)), _const_spec((1, wa)),
                  _const_spec(ones_bd.shape), _const_spec(w_s.shape), _const_spec(bias_exp.shape)],
        out_specs=[row(wa), row(wb), row(wb), row(wb), lane_major, lane_major],
        out_shape=[jax.ShapeDtypeStruct((b, s, wa), BF16), jax.ShapeDtypeStruct((b, s, wb), BF16),
                   jax.ShapeDtypeStruct((b, s, wb), F32), jax.ShapeDtypeStruct((b, s, wb), F32),
                   jax.ShapeDtypeStruct((b, nh, s), F32), jax.ShapeDtypeStruct((b, nh, s), F32)],
        scratch_shapes=[pltpu.VMEM((nh, LANES), F32)],
        compiler_params=_params(("arbitrary", "arbitrary")),
        name="mix_in_prompt",
    )(x, mods, mods, g_pre, w_main, wf_t, b_f, g_sgu, ones_bd, w_s, bias_exp)


def _mix_in_sample(x, mods, g_pre, w_main, wf_t, b_f, g_sgu, ones_bd):
    n, d = x.shape
    tm = mods.shape[0]
    wa = ones_bd.shape[0]
    wb = (w_main.shape[1] - 2 * wa) // 3
    nh = b_f.shape[0]
    row = lambda width: pl.BlockSpec((tm, width), lambda i: (i, 0))
    mod = lambda j: pl.BlockSpec((tm, d), lambda i: (0, j))
    return pl.pallas_call(
        _mix_in_sample_kernel,
        grid=(n // tm,),
        in_specs=[row(d), mod(0), mod(1), _const_spec((1, d)), _const_spec(w_main.shape),
                  _const_spec(wf_t.shape), _const_spec((nh, 1)), _const_spec((1, wa)),
                  _const_spec(ones_bd.shape)],
        out_specs=[row(wa), row(wa), row(wb), row(wb), row(wb),
                   pl.BlockSpec((nh, tm), lambda i: (0, i))],
        out_shape=[jax.ShapeDtypeStruct((n, wa), F32), jax.ShapeDtypeStruct((n, wa), F32),
                   jax.ShapeDtypeStruct((n, wb), BF16), jax.ShapeDtypeStruct((n, wb), F32),
                   jax.ShapeDtypeStruct((n, wb), F32), jax.ShapeDtypeStruct((nh, n), F32)],
        compiler_params=_params(("arbitrary",)),
        name="mix_in_sample",
    )(x, mods, mods, g_pre, w_main, wf_t, b_f, g_sgu, ones_bd)


def _sgu_sample_kernel(u_ref, va_ref, w_ref, b_ref, a_ref):
    t_new = u_ref.shape[0]
    for t in range(t_new):
        mixed = b_ref[t:t + 1, :]
        for j in range(t + 1):
            mixed = mixed + w_ref[t * t_new + j:t * t_new + j + 1, :] * va_ref[j]
        a_ref[t] = (u_ref[t] * mixed).astype(BF16)


def _sgu_sample(u, va, wvec, bvec):
    return pl.pallas_call(
        _sgu_sample_kernel,
        out_shape=jax.ShapeDtypeStruct(u.shape, BF16),
        name="sgu_sample",
    )(u, va, wvec, bvec)


def _flash_kernel(q_ref, k_ref, v_ref, f_ref, o_ref):
    tq = q_ref.shape[0]
    i = pl.program_id(2)
    q = q_ref[...]
    first = lax.broadcasted_iota(jnp.int32, (tq, LANES), 1) < HEAD_DIM
    zero = jnp.zeros_like(q)
    qs = (jnp.where(first, q, zero), jnp.where(first, zero, q))

    def step(j, carry, masked):
        start = pl.multiple_of(j * tq, tq)
        kb = k_ref[pl.ds(start, tq), :].astype(BF16)
        vb = v_ref[pl.ds(start, tq), :].astype(BF16)
        out = []
        for hh in range(2):
            m, l, acc = carry[hh]
            s = lax.dot_general(qs[hh], kb, _NT, preferred_element_type=F32)
            s = s - f_ref[hh, :, pl.ds(start, tq)]
            if masked:
                r = lax.broadcasted_iota(jnp.int32, (tq, tq), 0)
                c = lax.broadcasted_iota(jnp.int32, (tq, tq), 1)
                s = jnp.where(c <= r, s, NEG)
            m_new = jnp.maximum(m, jnp.max(s, axis=1, keepdims=True))
            alpha = jnp.exp(m - m_new)
            p = jnp.exp(s - m_new)
            l = alpha * l + jnp.sum(p, axis=1, keepdims=True)
            acc = alpha * acc + _dot(p.astype(BF16), vb)
            out.append((m_new, l, acc))
        return tuple(out)

    init = tuple((jnp.full((tq, 1), NEG, F32), jnp.zeros((tq, 1), F32), jnp.zeros((tq, LANES), F32))
                 for _ in range(2))
    carry = lax.fori_loop(0, i, lambda j, c: step(j, c, False), init)
    (m0, l0, a0), (m1, l1, a1) = step(i, carry, True)
    o_ref[...] = jnp.where(first, a0 / l0, a1 / l1).astype(BF16)


def _flash(q, k, v, fcum, tq=256):
    b, s, wb = q.shape
    nh = fcum.shape[1]
    f4 = fcum.reshape(b, nh, 1, s)
    blk = pl.BlockSpec((None, tq, LANES), lambda bi, hp, i: (bi, i, hp))
    full = pl.BlockSpec((None, s, LANES), lambda bi, hp, i: (bi, 0, hp))
    return pl.pallas_call(
        _flash_kernel,
        grid=(b, wb // LANES, s // tq),
        in_specs=[blk, full, full,
                  pl.BlockSpec((None, 2, 1, s), lambda bi, hp, i: (bi, hp, 0, 0))],
        out_specs=blk,
        out_shape=jax.ShapeDtypeStruct((b, s, wb), BF16),
        compiler_params=_params(("arbitrary", "arbitrary", "arbitrary")),
        name="fox_prompt",
    )(q, k, v, f4)


def _paged_kernel(pt_ref, q_ref, kn_ref, vn_ref, lf_ref, ck_ref, cv_ref, cf_ref, o_ref,
                  kbuf, vbuf, fbuf, rbuf, ksem, vsem, fsem, *, layer):
    n_pages = pt_ref.shape[1]
    page, n_heads, _ = kbuf.shape[2:]
    rows = q_ref.shape[0]
    width = page * n_heads
    b = pl.program_id(0)
    slot = lax.rem(b, 2)

    def copies(bb, sl):
        out = []
        for p in range(n_pages):
            pg = pt_ref[bb, p]
            out.append(pltpu.make_async_copy(ck_ref.at[pg, layer], kbuf.at[sl, p], ksem.at[sl]))
            out.append(pltpu.make_async_copy(cv_ref.at[pg, layer], vbuf.at[sl, p], vsem.at[sl]))
            out.append(pltpu.make_async_copy(cf_ref.at[pg, layer], fbuf.at[sl, p], fsem.at[sl]))
        return out

    @pl.when(b == 0)
    def _():
        for cp in copies(0, 0):
            cp.start()

    @pl.when(b + 1 < pl.num_programs(0))
    def _():
        for cp in copies(b + 1, 1 - slot):
            cp.start()

    for cp in copies(b, slot):
        cp.wait()

    f = fbuf[slot].reshape(n_pages * SUBLANES, LANES)
    lane = lax.broadcasted_iota(jnp.int32, f.shape, 1)
    incl = f
    sh = n_heads
    while sh < LANES:
        incl = incl + jnp.where(lane + sh < LANES, pltpu.roll(incl, LANES - sh, axis=1), 0.0)
        sh *= 2
    tot = jnp.where(lane < n_heads, incl, 0.0)
    sh = n_heads
    while sh < LANES:
        tot = tot + pltpu.roll(tot, sh, axis=1)
        sh *= 2
    nr = f.shape[0]
    later = (lax.broadcasted_iota(jnp.int32, (nr, nr), 1) >
             lax.broadcasted_iota(jnp.int32, (nr, nr), 0))
    su = jnp.where(later, 1.0, 0.0).astype(BF16)
    t1, t2, t3 = _split3(tot)
    rbuf[...] = incl - f + _dot(su, t1) + _dot(su, t2) + _dot(su, t3)

    q = q_ref[...]
    r_i = lax.broadcasted_iota(jnp.int32, (rows, width), 0)
    c_i = lax.broadcasted_iota(jnp.int32, (rows, width), 1)
    same_head = _mod(r_i, n_heads) == _mod(c_i, n_heads)
    groups = width // LANES

    def page_step(p, carry):
        m, l, acc = carry
        kp = kbuf[slot, p].reshape(width, HEAD_DIM).astype(BF16)
        vp = vbuf[slot, p].reshape(width, HEAD_DIM).astype(BF16)
        s = lax.dot_general(q, kp, _NT, preferred_element_type=F32)
        bias = jnp.concatenate(
            [rbuf[pl.ds(p * groups + g, 1), :] for g in range(groups)], axis=1)
        s = jnp.where(same_head, s + bias, NEG)
        m_new = jnp.maximum(m, jnp.max(s, axis=1, keepdims=True))
        alpha = jnp.exp(m - m_new)
        pr = jnp.exp(s - m_new)
        l = alpha * l + jnp.sum(pr, axis=1, keepdims=True)
        acc = alpha * acc + _dot(pr.astype(BF16), vp)
        return m_new, l, acc

    init = (jnp.full((rows, 1), NEG, F32), jnp.zeros((rows, 1), F32),
            jnp.zeros((rows, HEAD_DIM), F32))
    m, l, acc = lax.fori_loop(0, n_pages, page_step, init)

    lanes_n = lf_ref.shape[1]
    ln = lax.broadcasted_iota(jnp.int32, (1, lanes_n), 1)
    g = lf_ref[...]
    sh = n_heads
    while sh < rows:
        g = g + jnp.where(ln >= sh, pltpu.roll(g, sh, axis=1), 0.0)
        sh *= 2
    kn = kn_ref[...].astype(BF16)
    vn = vn_ref[...].astype(BF16)
    s = lax.dot_general(q, kn, _NT, preferred_element_type=F32) - g[:, :rows]
    rn = lax.broadcasted_iota(jnp.int32, (rows, rows), 0)
    cn = lax.broadcasted_iota(jnp.int32, (rows, rows), 1)
    ok = (_mod(rn, n_heads) == _mod(cn, n_heads)) & (_div(cn, n_heads) <= _div(rn, n_heads))
    s = jnp.where(ok, s, NEG)
    m_new = jnp.maximum(m, jnp.max(s, axis=1, keepdims=True))
    alpha = jnp.exp(m - m_new)
    pr = jnp.exp(s - m_new)
    l = alpha * l + jnp.sum(pr, axis=1, keepdims=True)
    acc = alpha * acc + _dot(pr.astype(BF16), vn)
    o_ref[...] = (acc / l).astype(BF16)


def _paged(page_table, q, kn, vn, lf, cache_k, cache_v, cache_f, layer):
    db, rows, hd = q.shape
    n_pages = page_table.shape[1]
    page, n_heads = cache_k.shape[2], cache_k.shape[3]
    per = lambda shape: pl.BlockSpec((None,) + shape, lambda b, pt: (b, 0, 0))
    any_spec = pl.BlockSpec(memory_space=pl.ANY)
    grid_spec = pltpu.PrefetchScalarGridSpec(
        num_scalar_prefetch=1,
        grid=(db,),
        in_specs=[per((rows, hd)), per((rows, hd)), per((rows, hd)), per((1, lf.shape[2])),
                  any_spec, any_spec, any_spec],
        out_specs=per((rows, hd)),
        scratch_shapes=[
            pltpu.VMEM((2, n_pages, page, n_heads, hd), F32),
            pltpu.VMEM((2, n_pages, page, n_heads, hd), F32),
            pltpu.VMEM((2, n_pages, SUBLANES, LANES), F32),
            pltpu.VMEM((n_pages * SUBLANES, LANES), F32),
            pltpu.SemaphoreType.DMA((2,)),
            pltpu.SemaphoreType.DMA((2,)),
            pltpu.SemaphoreType.DMA((2,)),
        ],
    )
    return pl.pallas_call(
        functools.partial(_paged_kernel, layer=layer),
        grid_spec=grid_spec,
        out_shape=jax.ShapeDtypeStruct((db, rows, hd), BF16),
        compiler_params=_params(("arbitrary",)),
        name="fox_sample",
    )(page_table, q, kn, vn, lf, cache_k, cache_v, cache_f)


def _tail_kernel(a_ref, o_ref, x_ref, gate1_ref, shift_ref, scale_ref, gate2_ref,
                 gpm_ref, gpf_ref, gqf_ref, woa_ref, woo_ref, wgu_ref, wdn_ref, y_ref, *, ff_chunk):
    d_ff = wdn_ref.shape[0]
    y = _dot(a_ref[...], woa_ref[...]) + _dot(o_ref[...], woo_ref[...])
    x1 = x_ref[...] + gate1_ref[...] * _rms(y, gpm_ref[...])
    h = (_rms(x1, gpf_ref[...]) * (1.0 + scale_ref[...]) + shift_ref[...]).astype(BF16)
    f = None
    for c0 in range(0, d_ff, ff_chunk):
        gte = _dot(h, wgu_ref[:, c0:c0 + ff_chunk])
        up = _dot(h, wgu_ref[:, d_ff + c0:d_ff + c0 + ff_chunk])
        t = (gte * jax.nn.sigmoid(gte) * up).astype(BF16)
        part = _dot(t, wdn_ref[c0:c0 + ff_chunk, :])
        f = part if f is None else f + part
    y_ref[...] = x1 + gate2_ref[...] * _rms(f, gqf_ref[...])


def _ff_chunk(d_ff):
    for c in (512, 256, 128):
        if d_ff % c == 0:
            return c
    return d_ff


def _tail_prompt(a, o, x, mods, g_post_mix, g_pre_ffn, g_post_ffn, woa, woo, wgu, wdn, tm=256):
    b, s, d = x.shape
    row = lambda width: pl.BlockSpec((None, tm, width), lambda bi, i: (bi, i, 0))
    mod = lambda j: pl.BlockSpec((None, 1, d), lambda bi, i: (bi, 0, j))
    return pl.pallas_call(
        functools.partial(_tail_kernel, ff_chunk=_ff_chunk(wdn.shape[0])),
        grid=(b, s // tm),
        in_specs=[row(a.shape[2]), row(o.shape[2]), row(d), mod(2), mod(3), mod(4), mod(5),
                  _const_spec((1, d)), _const_spec((1, d)), _const_spec((1, d)),
                  _const_spec(woa.shape), _const_spec(woo.shape), _const_spec(wgu.shape),
                  _const_spec(wdn.shape)],
        out_specs=row(d),
        out_shape=jax.ShapeDtypeStruct((b, s, d), F32),
        compiler_params=_params(("arbitrary", "arbitrary")),
        name="tail_prompt",
    )(a, o, x, mods, mods, mods, mods, g_post_mix, g_pre_ffn, g_post_ffn, woa, woo, wgu, wdn)


def _tail_sample(a, o, x, mods, g_post_mix, g_pre_ffn, g_post_ffn, woa, woo, wgu, wdn):
    n, d = x.shape
    tm = mods.shape[0]
    row = lambda width: pl.BlockSpec((tm, width), lambda i: (i, 0))
    mod = lambda j: pl.BlockSpec((tm, d), lambda i: (0, j))
    return pl.pallas_call(
        functools.partial(_tail_kernel, ff_chunk=_ff_chunk(wdn.shape[0])),
        grid=(n // tm,),
        in_specs=[row(a.shape[1]), row(o.shape[1]), row(d), mod(2), mod(3), mod(4), mod(5),
                  _const_spec((1, d)), _const_spec((1, d)), _const_spec((1, d)),
                  _const_spec(woa.shape), _const_spec(woo.shape), _const_spec(wgu.shape),
                  _const_spec(wdn.shape)],
        out_specs=row(d),
        out_shape=jax.ShapeDtypeStruct((n, d), F32),
        compiler_params=_params(("arbitrary",)),
        name="tail_sample",
    )(a, o, x, mods, mods, mods, mods, g_post_mix, g_pre_ffn, g_post_ffn, woa, woo, wgu, wdn)


def kernel(x_prompt, x_sample, c_prompt, c_sample, cache_k, cache_v, cache_logf, page_table, w_ada, b_ada, g_pre_mix, g_post_mix, g_pre_ffn, g_post_ffn, w_in, b_f, g_sgu, w_s, b_s, w_out, w_gu, w_down):
    bsz, seq, d = x_prompt.shape
    db, t_new, _ = x_sample.shape
    depth = w_ada.shape[0]
    n_phys, _, page, hb, hd = cache_k.shape
    ha = w_s.shape[1]
    wa, wb = ha * hd, hb * hd
    d_ff = w_down.shape[1]
    assert hd == HEAD_DIM and page == CHUNK and w_s.shape[2] == CHUNK
    npp = seq // page

    mods = _ada(jnp.concatenate([c_prompt, c_sample], axis=0), w_ada, b_ada)
    mods_p = mods[:, :bsz].reshape(depth, bsz, 1, 6 * d)
    mods_s = mods[:, bsz:]

    o_main = 2 * wa + 3 * wb
    w_main = w_in[:, :, :o_main].astype(BF16)
    wf_t = jnp.swapaxes(w_in[:, :, o_main:], 1, 2)
    wf_t = jnp.pad(wf_t, ((0, 0), (0, 2 * SUBLANES - hb), (0, 0))).astype(BF16)
    woa = w_out[:, :wa].astype(BF16)
    woo = w_out[:, wa:].astype(BF16)
    wgu = w_gu.astype(BF16)
    wdn = w_down.astype(BF16)
    grp = jnp.arange(wa) // hd
    ones_bd = (grp[:, None] == grp[None, :]).astype(BF16)
    bias_exp = jnp.repeat(jnp.swapaxes(b_s, 1, 2), hd, axis=2)
    wvec = jnp.repeat(jnp.transpose(w_s[:, :, :t_new, :t_new], (0, 2, 3, 1)), hd, axis=3)
    wvec = wvec.reshape(depth, t_new * t_new, wa)
    bvec = bias_exp[:, :t_new]
    cache_f = cache_logf.reshape(n_phys, depth, SUBLANES, (page * hb) // SUBLANES)

    xp = x_prompt
    xs = jnp.swapaxes(x_sample, 0, 1).reshape(t_new * db, d)
    pk, pv, pf, sk, sv, sf, sa = [], [], [], [], [], [], []
    for l in range(depth):
        row = lambda g: g[l].reshape(1, -1)
        a, q, k, v, lf, fcum = _mix_in_prompt(
            xp, mods_p[l], row(g_pre_mix), w_main[l], wf_t[l], b_f[l].reshape(hb, 1), row(g_sgu),
            ones_bd, w_s[l], bias_exp[l])
        o = _flash(q, k, v, fcum)
        xp = _tail_prompt(a, o, xp, mods_p[l], row(g_post_mix), row(g_pre_ffn), row(g_post_ffn),
                          woa[l], woo[l], wgu[l], wdn[l])
        pk.append(k); pv.append(v); pf.append(lf)

        u, va, q, k, v, lf = _mix_in_sample(
            xs, mods_s[l], row(g_pre_mix), w_main[l], wf_t[l], b_f[l].reshape(hb, 1), row(g_sgu),
            ones_bd)
        a = _sgu_sample(u.reshape(t_new, db, wa), va.reshape(t_new, db, wa), wvec[l], bvec[l])
        per_sample = lambda t: jnp.swapaxes(t.reshape(t_new, db, hb, hd), 0, 1)
        k5, v5 = per_sample(k), per_sample(v)
        lf3 = jnp.transpose(lf.reshape(hb, t_new, db), (2, 1, 0))
        lf_flat = jnp.pad(lf3.reshape(db, 1, t_new * hb), ((0, 0), (0, 0), (0, LANES - t_new * hb)))
        o = _paged(page_table, per_sample(q).reshape(db, t_new * hb, hd),
                   k5.reshape(db, t_new * hb, hd), v5.reshape(db, t_new * hb, hd), lf_flat,
                   cache_k, cache_v, cache_f, l)
        o = jnp.swapaxes(o.reshape(db, t_new, wb), 0, 1).reshape(t_new * db, wb)
        xs = _tail_sample(a.reshape(t_new * db, wa), o, xs, mods_s[l], row(g_post_mix),
                          row(g_pre_ffn), row(g_post_ffn), woa[l], woo[l], wgu[l], wdn[l])
        sk.append(k5); sv.append(v5); sf.append(lf3)
        sa.append(per_sample(va))

    paged = lambda ts: jnp.stack([t.reshape(bsz, npp, page, hb, hd) for t in ts], axis=2)
    prompt_k = paged(pk)
    prompt_v = paged(pv)
    prompt_logf = jnp.stack(
        [jnp.swapaxes(t, 1, 2).reshape(bsz, npp, page, hb) for t in pf], axis=2)
    y_sample = jnp.swapaxes(xs.reshape(t_new, db, d), 0, 1)
    return (xp, y_sample, prompt_k, prompt_v, prompt_logf, jnp.stack(sk, axis=1),
            jnp.stack(sv, axis=1), jnp.stack(sf, axis=1), jnp.stack(sa, axis=1))
```

```python
import functools

import jax
import jax.numpy as jnp
from jax import lax
from jax.experimental import pallas as pl
from jax.experimental.pallas import tpu as pltpu

F32 = jnp.float32
BF16 = jnp.bfloat16

HEAD_DIM = 64
CHUNK = 128
EPS = 1e-6
LANES = 128
SUBLANES = 8
BF16_ROWS = 16
NEG = -1e30
VMEM_LIMIT = 56 * 1024 * 1024

_NT = (((1,), (1,)), ((), ()))


def _params(sem, vmem=VMEM_LIMIT):
    return pltpu.CompilerParams(dimension_semantics=sem, vmem_limit_bytes=vmem)


def _rms(x, g):
    return x * lax.rsqrt(jnp.mean(x * x, axis=-1, keepdims=True) + EPS) * g


def _dot(a, b):
    return jnp.dot(a, b, preferred_element_type=F32)


def _dot_nt(a, b):
    return lax.dot_general(a, b, _NT, preferred_element_type=F32)


def _mod(x, n):
    return jnp.bitwise_and(x, n - 1) if n & (n - 1) == 0 else x % n


def _div(x, n):
    return lax.shift_right_logical(x, n.bit_length() - 1) if n & (n - 1) == 0 else x // n


def _split3(x):
    x1 = x.astype(BF16)
    r1 = x - x1.astype(F32)
    x2 = r1.astype(BF16)
    x3 = (r1 - x2.astype(F32)).astype(BF16)
    return x1, x2, x3


def _log_sigmoid(x):
    return jnp.minimum(x, 0.0) - jnp.log1p(jnp.exp(-jnp.abs(x)))


def _const_spec(shape):
    nd = len(shape)
    return pl.BlockSpec(shape, lambda *_: (0,) * nd, pipeline_mode=pl.Buffered(1))


def _ada_kernel(c_ref, w_ref, b_ref, o_ref):
    c = c_ref[...]
    s = (c * jax.nn.sigmoid(c)).astype(BF16)
    o_ref[...] = _dot(s, w_ref[...].astype(BF16)) + b_ref[...]


def _ada(c_all, w_ada, b_ada, tn=1024):
    depth, d, e = w_ada.shape
    n = c_all.shape[0]
    return pl.pallas_call(
        _ada_kernel,
        grid=(depth, e // tn),
        in_specs=[
            pl.BlockSpec((n, d), lambda l, j: (0, 0)),
            pl.BlockSpec((None, d, tn), lambda l, j: (l, 0, j)),
            pl.BlockSpec((None, 1, tn), lambda l, j: (l, 0, j)),
        ],
        out_specs=pl.BlockSpec((None, n, tn), lambda l, j: (l, 0, j)),
        out_shape=jax.ShapeDtypeStruct((depth, n, e), F32),
        compiler_params=_params(("arbitrary", "arbitrary")),
        name="ada",
    )(c_all, w_ada, b_ada.reshape(depth, 1, e))


def _norm_mod(x_ref, shift_ref, scale_ref, g_ref):
    h = _rms(x_ref[...], g_ref[...]) * (1.0 + scale_ref[...]) + shift_ref[...]
    return h.astype(BF16)


def _gates(z, wa, gsgu_ref, ones_ref):
    u = jax.nn.gelu(z[:, :wa])
    va = jax.nn.gelu(z[:, wa:2 * wa])
    gs = _dot((va * va).astype(BF16), ones_ref[...])
    va = va * lax.rsqrt(gs * (1.0 / HEAD_DIM) + EPS) * gsgu_ref[...]
    return u, va


def _logf_t(wf_ref, bf_ref, hb):
    nh = bf_ref.shape[0]
    return _log_sigmoid(_dot_nt(wf_ref[...], hb)[:nh] + bf_ref[...])


def _mix_in_prompt_kernel(x_ref, shift_ref, scale_ref, g_ref, w_ref, wkv_ref, wf_ref, bf_ref,
                          gsgu_ref, ones_ref, ws_ref, bias_ref, kin_ref, vin_ref,
                          a_ref, q_ref, k_ref, v_ref, lf_ref, fc_ref, carry_ref):
    del kin_ref, vin_ref
    tm = x_ref.shape[0]
    wa = ones_ref.shape[0]
    hb = _norm_mod(x_ref, shift_ref, scale_ref, g_ref)
    z = _dot(hb, w_ref[...])
    u, va = _gates(z, wa, gsgu_ref, ones_ref)
    q_ref[...] = (z[:, 2 * wa:] * (HEAD_DIM ** -0.5)).astype(BF16)

    kv_t = _dot_nt(wkv_ref[...], hb)
    wb = kv_t.shape[0] // 2
    for pg in range(tm // CHUNK):
        cols = slice(pg * CHUNK, (pg + 1) * CHUNK)
        k_ref[pg] = kv_t[:wb, cols].reshape(k_ref.shape[1:])
        v_ref[pg] = kv_t[wb:, cols].reshape(v_ref.shape[1:])

    logf_t = _logf_t(wf_ref, bf_ref, hb)
    lf_ref[...] = logf_t

    @pl.when(pl.program_id(1) == 0)
    def _():
        carry_ref[...] = jnp.zeros_like(carry_ref)

    r = lax.broadcasted_iota(jnp.int32, (tm, tm), 0)
    c = lax.broadcasted_iota(jnp.int32, (tm, tm), 1)
    tri = jnp.where(r <= c, 1.0, 0.0).astype(BF16)
    l1, l2, l3 = _split3(logf_t)
    fc = _dot(l1, tri) + _dot(l2, tri) + _dot(l3, tri) + carry_ref[:, :1]
    fc_ref[...] = fc
    carry_ref[...] = jnp.broadcast_to(fc[:, tm - 1:tm], carry_ref.shape)

    n_heads = ws_ref.shape[0]
    rr = lax.broadcasted_iota(jnp.int32, (CHUNK, CHUNK), 0)
    cc = lax.broadcasted_iota(jnp.int32, (CHUNK, CHUNK), 1)
    low = cc <= rr
    first = lax.broadcasted_iota(jnp.int32, (CHUNK, LANES), 1) < HEAD_DIM
    wts = [jnp.where(low, ws_ref[hh], 0.0).astype(BF16) for hh in range(n_heads)]
    for ci in range(tm // CHUNK):
        rows = slice(ci * CHUNK, (ci + 1) * CHUNK)
        for p in range(n_heads // 2):
            cols = slice(p * LANES, (p + 1) * LANES)
            vp = va[rows, cols].astype(BF16)
            mixed = jnp.where(first, _dot(wts[2 * p], vp), _dot(wts[2 * p + 1], vp))
            a_ref[rows, cols] = (u[rows, cols] * (mixed + bias_ref[:, cols])).astype(BF16)


def _mix_in_prompt(x, mods, g_pre, w_uvq, w_kv_t, wf_t, b_f, g_sgu, ones_bd, w_s, bias_exp,
                   k_all, v_all, layer, tm=512):
    b, s, d = x.shape
    wa = ones_bd.shape[0]
    wb = w_uvq.shape[1] - 2 * wa
    nh = b_f.shape[0]
    ppt = tm // CHUNK
    row = lambda width: pl.BlockSpec((None, tm, width), lambda bi, i: (bi, i, 0))
    mod = lambda j: pl.BlockSpec((None, 1, d), lambda bi, i: (bi, 0, j))
    lane_major = pl.BlockSpec((None, nh, tm), lambda bi, i: (bi, 0, i))
    kv_out = pl.BlockSpec((None, ppt, None) + k_all.shape[3:],
                          lambda bi, i: (bi, i, layer, 0, 0, 0))
    any_spec = pl.BlockSpec(memory_space=pl.ANY)
    ins = [x, mods, mods, g_pre, w_uvq, w_kv_t, wf_t, b_f, g_sgu, ones_bd, w_s, bias_exp,
           k_all, v_all]
    return pl.pallas_call(
        _mix_in_prompt_kernel,
        grid=(b, s // tm),
        in_specs=[row(d), mod(0), mod(1), _const_spec((1, d)), _const_spec(w_uvq.shape),
                  _const_spec(w_kv_t.shape), _const_spec(wf_t.shape), _const_spec((nh, 1)),
                  _const_spec((1, wa)), _const_spec(ones_bd.shape), _const_spec(w_s.shape),
                  _const_spec(bias_exp.shape), any_spec, any_spec],
        out_specs=[row(wa), row(wb), kv_out, kv_out, lane_major, lane_major],
        out_shape=[jax.ShapeDtypeStruct((b, s, wa), BF16), jax.ShapeDtypeStruct((b, s, wb), BF16),
                   jax.ShapeDtypeStruct(k_all.shape, F32), jax.ShapeDtypeStruct(v_all.shape, F32),
                   jax.ShapeDtypeStruct((b, nh, s), F32), jax.ShapeDtypeStruct((b, nh, s), F32)],
        scratch_shapes=[pltpu.VMEM((nh, LANES), F32)],
        input_output_aliases={len(ins) - 2: 2, len(ins) - 1: 3},
        compiler_params=_params(("arbitrary", "arbitrary")),
        name="mix_in_prompt",
    )(*ins)


def _mix_in_sample_kernel(x_ref, shift_ref, scale_ref, g_ref, w_ref, wf_ref, bf_ref, gsgu_ref,
                          ones_ref, u_ref, va_ref, q_ref, k_ref, v_ref, lf_ref):
    wa = ones_ref.shape[0]
    hb = _norm_mod(x_ref, shift_ref, scale_ref, g_ref)
    z = _dot(hb, w_ref[...])
    u, va = _gates(z, wa, gsgu_ref, ones_ref)
    u_ref[...] = u
    va_ref[...] = va
    o = 2 * wa
    wb = (z.shape[1] - o) // 3
    q_ref[...] = z[:, o:o + wb] * (HEAD_DIM ** -0.5)
    k_ref[...] = z[:, o + wb:o + 2 * wb]
    v_ref[...] = z[:, o + 2 * wb:o + 3 * wb]
    lf_ref[...] = _logf_t(wf_ref, bf_ref, hb)


def _mix_in_sample(x, mods, g_pre, w_main, wf_t, b_f, g_sgu, ones_bd):
    n, d = x.shape
    tm = mods.shape[0]
    wa = ones_bd.shape[0]
    wb = (w_main.shape[1] - 2 * wa) // 3
    nh = b_f.shape[0]
    row = lambda width: pl.BlockSpec((tm, width), lambda i: (i, 0))
    mod = lambda j: pl.BlockSpec((tm, d), lambda i: (0, j))
    return pl.pallas_call(
        _mix_in_sample_kernel,
        grid=(n // tm,),
        in_specs=[row(d), mod(0), mod(1), _const_spec((1, d)), _const_spec(w_main.shape),
                  _const_spec(wf_t.shape), _const_spec((nh, 1)), _const_spec((1, wa)),
                  _const_spec(ones_bd.shape)],
        out_specs=[row(wa), row(wa), row(wb), row(wb), row(wb),
                   pl.BlockSpec((nh, tm), lambda i: (0, i))],
        out_shape=[jax.ShapeDtypeStruct((n, wa), F32), jax.ShapeDtypeStruct((n, wa), F32),
                   jax.ShapeDtypeStruct((n, wb), F32), jax.ShapeDtypeStruct((n, wb), F32),
                   jax.ShapeDtypeStruct((n, wb), F32), jax.ShapeDtypeStruct((nh, n), F32)],
        compiler_params=_params(("arbitrary",)),
        name="mix_in_sample",
    )(x, mods, mods, g_pre, w_main, wf_t, b_f, g_sgu, ones_bd)


def _sgu_sample_kernel(u_ref, va_ref, w_ref, b_ref, a_ref):
    t_new = u_ref.shape[0]
    for t in range(t_new):
        mixed = b_ref[t:t + 1, :]
        for j in range(t + 1):
            mixed = mixed + w_ref[t * t_new + j:t * t_new + j + 1, :] * va_ref[j]
        a_ref[t] = (u_ref[t] * mixed).astype(BF16)


def _sgu_sample(u, va, wvec, bvec):
    return pl.pallas_call(
        _sgu_sample_kernel,
        out_shape=jax.ShapeDtypeStruct(u.shape, BF16),
        name="sgu_sample",
    )(u, va, wvec, bvec)


def _flash_kernel(q_ref, k_ref, v_ref, f_ref, o_ref):
    tq = q_ref.shape[0]
    ppb = tq // CHUNK
    i = pl.program_id(2)
    q = q_ref[...]
    first = lax.broadcasted_iota(jnp.int32, (tq, LANES), 1) < HEAD_DIM
    zero = jnp.zeros_like(q)
    qs = (jnp.where(first, q, zero), jnp.where(first, zero, q))

    def load_t(ref, j):
        return jnp.concatenate(
            [ref[j * ppb + p].reshape(LANES, CHUNK) for p in range(ppb)], axis=1).astype(BF16)

    def step(j, carry, masked):
        start = pl.multiple_of(j * tq, tq)
        k_t = load_t(k_ref, j)
        v_t = load_t(v_ref, j)
        out = []
        for hh in range(2):
            m, l, acc = carry[hh]
            s = _dot(qs[hh], k_t) - f_ref[hh, :, pl.ds(start, tq)]
            if masked:
                r = lax.broadcasted_iota(jnp.int32, (tq, tq), 0)
                c = lax.broadcasted_iota(jnp.int32, (tq, tq), 1)
                s = jnp.where(c <= r, s, NEG)
            m_new = jnp.maximum(m, jnp.max(s, axis=1, keepdims=True))
            alpha = jnp.exp(m - m_new)
            p = jnp.exp(s - m_new)
            l = alpha * l + jnp.sum(p, axis=1, keepdims=True)
            acc = alpha * acc + _dot_nt(p.astype(BF16), v_t)
            out.append((m_new, l, acc))
        return tuple(out)

    init = tuple((jnp.full((tq, 1), NEG, F32), jnp.zeros((tq, 1), F32), jnp.zeros((tq, LANES), F32))
                 for _ in range(2))
    carry = lax.fori_loop(0, i, lambda j, c: step(j, c, False), init)
    (_, l0, a0), (_, l1, a1) = step(i, carry, True)
    o_ref[...] = jnp.where(first, a0 / l0, a1 / l1).astype(BF16)


def _flash(q, k_all, v_all, fcum, layer, tq=512):
    b, s, wb = q.shape
    nh = fcum.shape[1]
    npp, hd, page = k_all.shape[1], k_all.shape[4], k_all.shape[5]
    f4 = fcum.reshape(b, nh, 1, s)
    blk = pl.BlockSpec((None, tq, LANES), lambda bi, hp, i: (bi, i, hp))
    pair = pl.BlockSpec((None, npp, None, 2, hd, page), lambda bi, hp, i: (bi, 0, layer, hp, 0, 0))
    return pl.pallas_call(
        _flash_kernel,
        grid=(b, wb // LANES, s // tq),
        in_specs=[blk, pair, pair,
                  pl.BlockSpec((None, 2, 1, s), lambda bi, hp, i: (bi, hp, 0, 0))],
        out_specs=blk,
        out_shape=jax.ShapeDtypeStruct((b, s, wb), BF16),
        compiler_params=_params(("arbitrary", "arbitrary", "arbitrary")),
        name="fox_prompt",
    )(q, k_all, v_all, f4)


def _paged_kernel(pt_ref, q_ref, kn_ref, vn_ref, lf_ref, ck_ref, cv_ref, cf_ref, o_ref,
                  kbuf, vbuf, fbuf, ksem, vsem, fsem, *, layer):
    n_pages = pt_ref.shape[1]
    n_heads, hd, page = kbuf.shape[2:]
    t_new, wb = q_ref.shape
    rows = t_new * n_heads
    b = pl.program_id(0)
    slot = lax.rem(b, 2)

    def copies(bb, sl):
        out = []
        for p in range(n_pages):
            pg = pt_ref[bb, p]
            out.append(pltpu.make_async_copy(ck_ref.at[pg, layer], kbuf.at[sl, p], ksem.at[sl]))
            out.append(pltpu.make_async_copy(cv_ref.at[pg, layer], vbuf.at[sl, p], vsem.at[sl]))
            out.append(pltpu.make_async_copy(cf_ref.at[pg, layer], fbuf.at[sl, p], fsem.at[sl]))
        return out

    @pl.when(b == 0)
    def _():
        for cp in copies(0, 0):
            cp.start()

    @pl.when(b + 1 < pl.num_programs(0))
    def _():
        for cp in copies(b + 1, 1 - slot):
            cp.start()

    for cp in copies(b, slot):
        cp.wait()

    f = fbuf[slot].reshape(n_pages * n_heads, page)
    lane = lax.broadcasted_iota(jnp.int32, f.shape, 1)
    incl = f
    sh = 1
    while sh < page:
        incl = incl + jnp.where(lane + sh < page, pltpu.roll(incl, page - sh, axis=1), 0.0)
        sh *= 2
    tot = jnp.broadcast_to(incl[:, :1], f.shape)
    nr = f.shape[0]
    rr = lax.broadcasted_iota(jnp.int32, (nr, nr), 0)
    cc = lax.broadcasted_iota(jnp.int32, (nr, nr), 1)
    later = (_mod(rr, n_heads) == _mod(cc, n_heads)) & (cc > rr)
    su = jnp.where(later, 1.0, 0.0).astype(BF16)
    t1, t2, t3 = _split3(tot)
    r_all = incl - f + _dot(su, t1) + _dot(su, t2) + _dot(su, t3)

    own = (lax.broadcasted_iota(jnp.int32, (n_heads, wb), 0) ==
           _div(lax.broadcasted_iota(jnp.int32, (n_heads, wb), 1), hd))
    q = q_ref[...]
    qbd = jnp.concatenate(
        [jnp.where(own, jnp.broadcast_to(q[t:t + 1, :], (n_heads, wb)), 0.0) for t in range(t_new)],
        axis=0).astype(BF16)

    s_past = []
    for p in range(n_pages):
        k_t = kbuf[slot, p].reshape(wb, page).astype(BF16)
        r8 = r_all[p * n_heads:(p + 1) * n_heads, :]
        s_past.append(_dot(qbd, k_t) + jnp.concatenate([r8] * t_new, axis=0))

    g = lf_ref[...]
    ln = lax.broadcasted_iota(jnp.int32, g.shape, 1)
    sh = 1
    while sh < t_new:
        g = g + jnp.where(ln >= sh, pltpu.roll(g, sh, axis=1), 0.0)
        sh *= 2
    n_pad = kn_ref.shape[0]
    g = jnp.concatenate([g[:, :n_pad]] * t_new, axis=0)
    s_new = _dot_nt(qbd, kn_ref[...].astype(BF16)) - g
    rn = lax.broadcasted_iota(jnp.int32, (rows, n_pad), 0)
    cn = lax.broadcasted_iota(jnp.int32, (rows, n_pad), 1)
    s_new = jnp.where(cn <= _div(rn, n_heads), s_new, NEG)

    m = jnp.max(s_new, axis=1, keepdims=True)
    m_el = s_past[0]
    for s in s_past[1:]:
        m_el = jnp.maximum(m_el, s)
    m = jnp.maximum(m, jnp.max(m_el, axis=1, keepdims=True))
    pr = jnp.exp(s_new - m)
    l = jnp.sum(pr, axis=1, keepdims=True)
    acc = _dot(pr.astype(BF16), vn_ref[...].astype(BF16))
    l_el = None
    for p in range(n_pages):
        pr = jnp.exp(s_past[p] - m)
        l_el = pr if l_el is None else l_el + pr
        acc = acc + _dot_nt(pr.astype(BF16), vbuf[slot, p].reshape(wb, page).astype(BF16))
    l = l + jnp.sum(l_el, axis=1, keepdims=True)

    own_rows = jnp.concatenate([own] * t_new, axis=0)
    o = jnp.where(own_rows, acc / l, 0.0)
    o_ref[...] = jnp.sum(o.reshape(t_new, n_heads, wb), axis=1)


def _paged(page_table, q, kn, vn, lf, cache_k, cache_v, cache_f, layer):
    db, t_new, wb = q.shape
    n_pages = page_table.shape[1]
    n_heads, hd, page = cache_k.shape[2:]
    per = lambda shape: pl.BlockSpec((None,) + shape, lambda b, pt: (b, 0, 0))
    any_spec = pl.BlockSpec(memory_space=pl.ANY)
    grid_spec = pltpu.PrefetchScalarGridSpec(
        num_scalar_prefetch=1,
        grid=(db,),
        in_specs=[per((t_new, wb)), per(kn.shape[1:]), per(vn.shape[1:]), per(lf.shape[1:]),
                  any_spec, any_spec, any_spec],
        out_specs=per((t_new, wb)),
        scratch_shapes=[
            pltpu.VMEM((2, n_pages, n_heads, hd, page), F32),
            pltpu.VMEM((2, n_pages, n_heads, hd, page), F32),
            pltpu.VMEM((2, n_pages, n_heads, page), F32),
            pltpu.SemaphoreType.DMA((2,)),
            pltpu.SemaphoreType.DMA((2,)),
            pltpu.SemaphoreType.DMA((2,)),
        ],
    )
    return pl.pallas_call(
        functools.partial(_paged_kernel, layer=layer),
        grid_spec=grid_spec,
        out_shape=jax.ShapeDtypeStruct((db, t_new, wb), F32),
        compiler_params=_params(("arbitrary",)),
        name="fox_sample",
    )(page_table, q, kn, vn, lf, cache_k, cache_v, cache_f)


def _tail_kernel(a_ref, o_ref, x_ref, gate1_ref, shift_ref, scale_ref, gate2_ref,
                 gpm_ref, gpf_ref, gqf_ref, woa_ref, woo_ref, wgu_ref, wdn_ref, y_ref, *, ff_chunk):
    d_ff = wdn_ref.shape[0]
    y = _dot(a_ref[...].astype(BF16), woa_ref[...]) + _dot(o_ref[...].astype(BF16), woo_ref[...])
    x1 = x_ref[...] + gate1_ref[...] * _rms(y, gpm_ref[...])
    h = (_rms(x1, gpf_ref[...]) * (1.0 + scale_ref[...]) + shift_ref[...]).astype(BF16)
    f = None
    for c0 in range(0, d_ff, ff_chunk):
        gte = _dot(h, wgu_ref[:, c0:c0 + ff_chunk])
        up = _dot(h, wgu_ref[:, d_ff + c0:d_ff + c0 + ff_chunk])
        t = (gte * jax.nn.sigmoid(gte) * up).astype(BF16)
        part = _dot(t, wdn_ref[c0:c0 + ff_chunk, :])
        f = part if f is None else f + part
    y_ref[...] = x1 + gate2_ref[...] * _rms(f, gqf_ref[...])


def _ff_chunk(d_ff):
    for c in (512, 256, 128):
        if d_ff % c == 0:
            return c
    return d_ff


def _tail_prompt(a, o, x, mods, g_post_mix, g_pre_ffn, g_post_ffn, woa, woo, wgu, wdn, tm=512):
    b, s, d = x.shape
    row = lambda width: pl.BlockSpec((None, tm, width), lambda bi, i: (bi, i, 0))
    mod = lambda j: pl.BlockSpec((None, 1, d), lambda bi, i: (bi, 0, j))
    return pl.pallas_call(
        functools.partial(_tail_kernel, ff_chunk=_ff_chunk(wdn.shape[0])),
        grid=(b, s // tm),
        in_specs=[row(a.shape[2]), row(o.shape[2]), row(d), mod(2), mod(3), mod(4), mod(5),
                  _const_spec((1, d)), _const_spec((1, d)), _const_spec((1, d)),
                  _const_spec(woa.shape), _const_spec(woo.shape), _const_spec(wgu.shape),
                  _const_spec(wdn.shape)],
        out_specs=row(d),
        out_shape=jax.ShapeDtypeStruct((b, s, d), F32),
        compiler_params=_params(("arbitrary", "arbitrary")),
        name="tail_prompt",
    )(a, o, x, mods, mods, mods, mods, g_post_mix, g_pre_ffn, g_post_ffn, woa, woo, wgu, wdn)


def _tail_sample(a, o, x, mods, g_post_mix, g_pre_ffn, g_post_ffn, woa, woo, wgu, wdn):
    n, d = x.shape
    tm = mods.shape[0]
    row = lambda width: pl.BlockSpec((tm, width), lambda i: (i, 0))
    mod = lambda j: pl.BlockSpec((tm, d), lambda i: (0, j))
    return pl.pallas_call(
        functools.partial(_tail_kernel, ff_chunk=_ff_chunk(wdn.shape[0])),
        grid=(n // tm,),
        in_specs=[row(a.shape[1]), row(o.shape[1]), row(d), mod(2), mod(3), mod(4), mod(5),
                  _const_spec((1, d)), _const_spec((1, d)), _const_spec((1, d)),
                  _const_spec(woa.shape), _const_spec(woo.shape), _const_spec(wgu.shape),
                  _const_spec(wdn.shape)],
        out_specs=row(d),
        out_shape=jax.ShapeDtypeStruct((n, d), F32),
        compiler_params=_params(("arbitrary",)),
        name="tail_sample",
    )(a, o, x, mods, mods, mods, mods, g_post_mix, g_pre_ffn, g_post_ffn, woa, woo, wgu, wdn)


def kernel(x_prompt, x_sample, c_prompt, c_sample, cache_k, cache_v, cache_logf, page_table, w_ada, b_ada, g_pre_mix, g_post_mix, g_pre_ffn, g_post_ffn, w_in, b_f, g_sgu, w_s, b_s, w_out, w_gu, w_down):
    bsz, seq, d = x_prompt.shape
    db, t_new, _ = x_sample.shape
    depth = w_ada.shape[0]
    _, _, page, hb, hd = cache_k.shape
    ha = w_s.shape[1]
    wa, wb = ha * hd, hb * hd
    assert hd == HEAD_DIM and page == CHUNK and w_s.shape[2] == CHUNK and 2 * hd == LANES
    npp = seq // page

    mods = _ada(jnp.concatenate([c_prompt, c_sample], axis=0), w_ada, b_ada)
    mods_p = mods[:, :bsz].reshape(depth, bsz, 1, 6 * d)
    mods_s = mods[:, bsz:]

    o_main = 2 * wa + 3 * wb
    w_main = w_in[:, :, :o_main].astype(BF16)
    w_uvq = w_main[:, :, :2 * wa + wb]
    w_kv_t = jnp.swapaxes(w_main[:, :, 2 * wa + wb:], 1, 2)
    wf_t = jnp.swapaxes(w_in[:, :, o_main:], 1, 2)
    wf_t = jnp.pad(wf_t, ((0, 0), (0, BF16_ROWS - hb), (0, 0))).astype(BF16)
    woa = w_out[:, :wa].astype(BF16)
    woo = w_out[:, wa:].astype(BF16)
    wgu = w_gu.astype(BF16)
    wdn = w_down.astype(BF16)
    grp = jnp.arange(wa) // hd
    ones_bd = (grp[:, None] == grp[None, :]).astype(BF16)
    bias_exp = jnp.repeat(jnp.swapaxes(b_s, 1, 2), hd, axis=2)
    wvec = jnp.repeat(jnp.transpose(w_s[:, :, :t_new, :t_new], (0, 2, 3, 1)), hd, axis=3)
    wvec = wvec.reshape(depth, t_new * t_new, wa)
    bvec = bias_exp[:, :t_new]
    ck_t = jnp.transpose(cache_k, (0, 1, 3, 4, 2))
    cv_t = jnp.transpose(cache_v, (0, 1, 3, 4, 2))
    cf_t = jnp.transpose(cache_logf, (0, 1, 3, 2))

    xp = x_prompt
    xs = jnp.swapaxes(x_sample, 0, 1).reshape(t_new * db, d)
    k_all = jnp.zeros((bsz, npp, depth, hb, hd, page), F32)
    v_all = jnp.zeros((bsz, npp, depth, hb, hd, page), F32)
    pf, sk, sv, sf, sa = [], [], [], [], []
    per_sample = lambda t: jnp.swapaxes(t.reshape(t_new, db, -1), 0, 1)
    pad_rows = lambda t: jnp.pad(t, ((0, 0), (0, BF16_ROWS - t_new), (0, 0)))
    for l in range(depth):
        row = lambda g: g[l].reshape(1, -1)
        a, q, k_all, v_all, lf, fcum = _mix_in_prompt(
            xp, mods_p[l], row(g_pre_mix), w_uvq[l], w_kv_t[l], wf_t[l], b_f[l].reshape(hb, 1),
            row(g_sgu), ones_bd, w_s[l], bias_exp[l], k_all, v_all, l)
        o = _flash(q, k_all, v_all, fcum, l)
        xp = _tail_prompt(a, o, xp, mods_p[l], row(g_post_mix), row(g_pre_ffn), row(g_post_ffn),
                          woa[l], woo[l], wgu[l], wdn[l])
        pf.append(lf)

        u, va, q, k, v, lf = _mix_in_sample(
            xs, mods_s[l], row(g_pre_mix), w_main[l], wf_t[l], b_f[l].reshape(hb, 1), row(g_sgu),
            ones_bd)
        a = _sgu_sample(u.reshape(t_new, db, wa), va.reshape(t_new, db, wa), wvec[l], bvec[l])
        k3, v3 = per_sample(k), per_sample(v)
        lf3 = jnp.transpose(lf.reshape(hb, t_new, db), (2, 0, 1))
        lf_pad = jnp.pad(lf3, ((0, 0), (0, 0), (0, LANES - t_new)))
        o = _paged(page_table, per_sample(q), pad_rows(k3), pad_rows(v3), lf_pad,
                   ck_t, cv_t, cf_t, l)
        xs = _tail_sample(a.reshape(t_new * db, wa), jnp.swapaxes(o, 0, 1).reshape(t_new * db, wb),
                          xs, mods_s[l], row(g_post_mix), row(g_pre_ffn), row(g_post_ffn),
                          woa[l], woo[l], wgu[l], wdn[l])
        sk.append(k3.reshape(db, t_new, hb, hd))
        sv.append(v3.reshape(db, t_new, hb, hd))
        sf.append(jnp.swapaxes(lf3, 1, 2))
        sa.append(per_sample(va).reshape(db, t_new, ha, hd))

    prompt_k = jnp.transpose(k_all, (0, 1, 2, 5, 3, 4))
    prompt_v = jnp.transpose(v_all, (0, 1, 2, 5, 3, 4))
    prompt_logf = jnp.stack(
        [jnp.swapaxes(t, 1, 2).reshape(bsz, npp, page, hb) for t in pf], axis=2)
    y_sample = jnp.swapaxes(xs.reshape(t_new, db, d), 0, 1)
    return (xp, y_sample, prompt_k, prompt_v, prompt_logf, jnp.stack(sk, axis=1),
            jnp.stack(sv, axis=1), jnp.stack(sf, axis=1), jnp.stack(sa, axis=1))
```

```python
import functools

import jax
import jax.numpy as jnp
from jax import lax
from jax.experimental import pallas as pl
from jax.experimental.pallas import tpu as pltpu

F32 = jnp.float32
BF16 = jnp.bfloat16

HEAD_DIM = 64
CHUNK = 128
EPS = 1e-6
LANES = 128
SUBLANES = 8
BF16_ROWS = 16
NEG = -1e30
LOG2E = 1.4426950408889634
VMEM_LIMIT = 56 * 1024 * 1024

_NT = (((1,), (1,)), ((), ()))


def _params(sem, vmem=VMEM_LIMIT):
    return pltpu.CompilerParams(dimension_semantics=sem, vmem_limit_bytes=vmem)


def _rms(x, g):
    return x * lax.rsqrt(jnp.mean(x * x, axis=-1, keepdims=True) + EPS) * g


def _dot(a, b):
    return jnp.dot(a, b, preferred_element_type=F32)


def _dot_nt(a, b):
    return lax.dot_general(a, b, _NT, preferred_element_type=F32)


def _mod(x, n):
    return jnp.bitwise_and(x, n - 1) if n & (n - 1) == 0 else x % n


def _div(x, n):
    return lax.shift_right_logical(x, n.bit_length() - 1) if n & (n - 1) == 0 else x // n


def _split3(x):
    x1 = x.astype(BF16)
    r1 = x - x1.astype(F32)
    x2 = r1.astype(BF16)
    x3 = (r1 - x2.astype(F32)).astype(BF16)
    return x1, x2, x3


def _log_sigmoid(x):
    return jnp.minimum(x, 0.0) - jnp.log1p(jnp.exp(-jnp.abs(x)))


def _layer_spec(arr, layer):
    nd = arr.ndim - 1
    return pl.BlockSpec((None,) + arr.shape[1:], lambda *_: (layer,) + (0,) * nd,
                        pipeline_mode=pl.Buffered(1))


def _const_spec(shape):
    nd = len(shape)
    return pl.BlockSpec(shape, lambda *_: (0,) * nd, pipeline_mode=pl.Buffered(1))


def _ada_kernel(c_ref, w_ref, b_ref, o_ref):
    c = c_ref[...]
    s = (c * jax.nn.sigmoid(c)).astype(BF16)
    o_ref[...] = _dot(s, w_ref[...].astype(BF16)) + b_ref[...]


def _ada(c_all, w_ada, b_ada, tn=1024):
    depth, d, e = w_ada.shape
    n = c_all.shape[0]
    return pl.pallas_call(
        _ada_kernel,
        grid=(depth, e // tn),
        in_specs=[
            pl.BlockSpec((n, d), lambda l, j: (0, 0)),
            pl.BlockSpec((None, d, tn), lambda l, j: (l, 0, j)),
            pl.BlockSpec((None, 1, tn), lambda l, j: (l, 0, j)),
        ],
        out_specs=pl.BlockSpec((None, n, tn), lambda l, j: (l, 0, j)),
        out_shape=jax.ShapeDtypeStruct((depth, n, e), F32),
        compiler_params=_params(("arbitrary", "arbitrary")),
        name="ada",
    )(c_all, w_ada, b_ada.reshape(depth, 1, e))


def _norm_mod(x_ref, shift_ref, scale_ref, g_ref):
    h = _rms(x_ref[...], g_ref[...]) * (1.0 + scale_ref[...]) + shift_ref[...]
    return h.astype(BF16)


def _gates(z, wa, gsgu_ref, ones_ref):
    u = jax.nn.gelu(z[:, :wa])
    va = jax.nn.gelu(z[:, wa:2 * wa])
    gs = _dot((va * va).astype(BF16), ones_ref[...])
    va = va * lax.rsqrt(gs * (1.0 / HEAD_DIM) + EPS) * gsgu_ref[...]
    return u, va


def _logf_t(wf_ref, bf_ref, hb):
    nh = bf_ref.shape[0]
    return _log_sigmoid(_dot_nt(wf_ref[...], hb)[:nh] + bf_ref[...])


def _mix_in_prompt_kernel(x_ref, shift_ref, scale_ref, g_ref, w_ref, wkv_ref, wf_ref, bf_ref,
                          gsgu_ref, ones_ref, ws_ref, bias_ref, kin_ref, vin_ref,
                          a_ref, q_ref, vn_ref, k_ref, v_ref, lf_ref, fc_ref, carry_ref):
    del kin_ref, vin_ref
    tm = x_ref.shape[0]
    wa = ones_ref.shape[0]
    hb = _norm_mod(x_ref, shift_ref, scale_ref, g_ref)
    z = _dot(hb, w_ref[...])
    u, va = _gates(z, wa, gsgu_ref, ones_ref)
    wb = (z.shape[1] - 2 * wa) // 2
    q_ref[...] = (z[:, 2 * wa:2 * wa + wb] * (HEAD_DIM ** -0.5 * LOG2E)).astype(BF16)
    vn_ref[...] = z[:, 2 * wa + wb:].astype(BF16)

    kv_t = _dot_nt(wkv_ref[...], hb)
    for pg in range(tm // CHUNK):
        cols = slice(pg * CHUNK, (pg + 1) * CHUNK)
        k_ref[pg] = kv_t[:wb, cols].reshape(k_ref.shape[1:])
        v_ref[pg] = kv_t[wb:, cols].reshape(v_ref.shape[1:])

    logf_t = _logf_t(wf_ref, bf_ref, hb)
    lf_ref[...] = logf_t

    @pl.when(pl.program_id(1) == 0)
    def _():
        carry_ref[...] = jnp.zeros_like(carry_ref)

    r = lax.broadcasted_iota(jnp.int32, (tm, tm), 0)
    c = lax.broadcasted_iota(jnp.int32, (tm, tm), 1)
    tri = jnp.where(r <= c, 1.0, 0.0).astype(BF16)
    l1, l2, l3 = _split3(logf_t)
    fc = _dot(l1, tri) + _dot(l2, tri) + _dot(l3, tri) + carry_ref[:, :1]
    fc_ref[...] = fc
    carry_ref[...] = jnp.broadcast_to(fc[:, tm - 1:tm], carry_ref.shape)

    n_heads = ws_ref.shape[0]
    rr = lax.broadcasted_iota(jnp.int32, (CHUNK, CHUNK), 0)
    cc = lax.broadcasted_iota(jnp.int32, (CHUNK, CHUNK), 1)
    low = cc <= rr
    first = lax.broadcasted_iota(jnp.int32, (CHUNK, LANES), 1) < HEAD_DIM
    wts = [jnp.where(low, ws_ref[hh], 0.0).astype(BF16) for hh in range(n_heads)]
    for ci in range(tm // CHUNK):
        rows = slice(ci * CHUNK, (ci + 1) * CHUNK)
        for p in range(n_heads // 2):
            cols = slice(p * LANES, (p + 1) * LANES)
            vp = va[rows, cols].astype(BF16)
            mixed = jnp.where(first, _dot(wts[2 * p], vp), _dot(wts[2 * p + 1], vp))
            a_ref[rows, cols] = (u[rows, cols] * (mixed + bias_ref[:, cols])).astype(BF16)


def _mix_in_prompt(x, mods, g_pre, w_uvqv, w_kv_t, wf_t, b_f, g_sgu, ones_bd, w_s, bias_exp,
                   k_all, v_all, layer, tm=512):
    b, s, d = x.shape
    wa = ones_bd.shape[0]
    wb = (w_uvqv.shape[2] - 2 * wa) // 2
    nh = b_f.shape[1]
    ppt = tm // CHUNK
    row = lambda width: pl.BlockSpec((None, tm, width), lambda bi, i: (bi, i, 0))
    mod = lambda j: pl.BlockSpec((None, None, 1, d), lambda bi, i: (layer, bi, 0, j))
    lane_major = pl.BlockSpec((None, nh, tm), lambda bi, i: (bi, 0, i))
    kv_out = pl.BlockSpec((None, ppt, None) + k_all.shape[3:],
                          lambda bi, i: (bi, i, layer, 0, 0, 0))
    any_spec = pl.BlockSpec(memory_space=pl.ANY)
    stacked = [g_pre, w_uvqv, w_kv_t, wf_t, b_f, g_sgu]
    ins = [x, mods, mods] + stacked + [ones_bd, w_s, bias_exp, k_all, v_all]
    return pl.pallas_call(
        _mix_in_prompt_kernel,
        grid=(b, s // tm),
        in_specs=[row(d), mod(0), mod(1)] + [_layer_spec(t, layer) for t in stacked] +
                 [_const_spec(ones_bd.shape), _layer_spec(w_s, layer), _layer_spec(bias_exp, layer),
                  any_spec, any_spec],
        out_specs=[row(wa), row(wb), row(wb), kv_out, kv_out, lane_major, lane_major],
        out_shape=[jax.ShapeDtypeStruct((b, s, wa), BF16), jax.ShapeDtypeStruct((b, s, wb), BF16),
                   jax.ShapeDtypeStruct((b, s, wb), BF16),
                   jax.ShapeDtypeStruct(k_all.shape, F32), jax.ShapeDtypeStruct(v_all.shape, F32),
                   jax.ShapeDtypeStruct((b, nh, s), F32), jax.ShapeDtypeStruct((b, nh, s), F32)],
        scratch_shapes=[pltpu.VMEM((nh, LANES), F32)],
        input_output_aliases={len(ins) - 2: 3, len(ins) - 1: 4},
        compiler_params=_params(("arbitrary", "arbitrary")),
        name="mix_in_prompt",
    )(*ins)


def _mix_in_sample_kernel(x_ref, shift_ref, scale_ref, g_ref, w_ref, wf_ref, bf_ref, gsgu_ref,
                          ones_ref, u_ref, va_ref, q_ref, k_ref, v_ref, lf_ref):
    wa = ones_ref.shape[0]
    hb = _norm_mod(x_ref, shift_ref, scale_ref, g_ref)
    z = _dot(hb, w_ref[...])
    u, va = _gates(z, wa, gsgu_ref, ones_ref)
    u_ref[...] = u
    va_ref[...] = va
    o = 2 * wa
    wb = (z.shape[1] - o) // 3
    q_ref[...] = z[:, o:o + wb] * (HEAD_DIM ** -0.5)
    k_ref[...] = z[:, o + wb:o + 2 * wb]
    v_ref[...] = z[:, o + 2 * wb:o + 3 * wb]
    lf_ref[...] = _logf_t(wf_ref, bf_ref, hb)


def _mix_in_sample(x, mods, g_pre, w_main, wf_t, b_f, g_sgu, ones_bd, layer):
    n, d = x.shape
    tm = mods.shape[1]
    wa = ones_bd.shape[0]
    wb = (w_main.shape[2] - 2 * wa) // 3
    nh = b_f.shape[1]
    row = lambda width: pl.BlockSpec((tm, width), lambda i: (i, 0))
    mod = lambda j: pl.BlockSpec((None, tm, d), lambda i: (layer, 0, j))
    stacked = [g_pre, w_main, wf_t, b_f, g_sgu]
    return pl.pallas_call(
        _mix_in_sample_kernel,
        grid=(n // tm,),
        in_specs=[row(d), mod(0), mod(1)] + [_layer_spec(t, layer) for t in stacked] +
                 [_const_spec(ones_bd.shape)],
        out_specs=[row(wa), row(wa), row(wb), row(wb), row(wb),
                   pl.BlockSpec((nh, tm), lambda i: (0, i))],
        out_shape=[jax.ShapeDtypeStruct((n, wa), F32), jax.ShapeDtypeStruct((n, wa), F32),
                   jax.ShapeDtypeStruct((n, wb), F32), jax.ShapeDtypeStruct((n, wb), F32),
                   jax.ShapeDtypeStruct((n, wb), F32), jax.ShapeDtypeStruct((nh, n), F32)],
        compiler_params=_params(("arbitrary",)),
        name="mix_in_sample",
    )(x, mods, mods, *stacked, ones_bd)


def _sgu_sample_kernel(u_ref, va_ref, w_ref, b_ref, a_ref):
    t_new = u_ref.shape[0]
    for t in range(t_new):
        mixed = b_ref[t:t + 1, :]
        for j in range(t + 1):
            mixed = mixed + w_ref[t * t_new + j:t * t_new + j + 1, :] * va_ref[j]
        a_ref[t] = (u_ref[t] * mixed).astype(BF16)


def _sgu_sample(u, va, wvec, bvec):
    return pl.pallas_call(
        _sgu_sample_kernel,
        out_shape=jax.ShapeDtypeStruct(u.shape, BF16),
        name="sgu_sample",
    )(u, va, wvec, bvec)


def _pieces3(x):
    x1 = x.astype(BF16).astype(F32)
    r1 = x - x1
    x2 = r1.astype(BF16).astype(F32)
    x3 = (r1 - x2).astype(BF16).astype(F32)
    return x1, x2, x3


def _flash_kernel(q_ref, k_ref, v_ref, f_ref, o_ref, *, tq):
    seq = q_ref.shape[0]
    ppb = tq // CHUNK
    lane = lax.broadcasted_iota(jnp.int32, (tq, LANES), 1)
    first = lane < HEAD_DIM
    row8 = lax.broadcasted_iota(jnp.int32, (SUBLANES, tq), 0)
    pad = jnp.zeros((HEAD_DIM - SUBLANES, tq), F32)
    causal = (lax.broadcasted_iota(jnp.int32, (tq, tq), 1) <=
              lax.broadcasted_iota(jnp.int32, (tq, tq), 0))

    k_aug, v_aug = [], []
    for j in range(seq // tq):
        k_t = jnp.concatenate(
            [k_ref[j * ppb + p].reshape(LANES, CHUNK) for p in range(ppb)], axis=1)
        v = v_ref[j * tq:(j + 1) * tq, :].astype(F32)
        ka, va = [], []
        for hh in range(2):
            x1, x2, x3 = _pieces3(f_ref[hh, :, j * tq:(j + 1) * tq] * LOG2E)
            fp = jnp.where(row8 == 0, x1, jnp.where(row8 == 1, x2, jnp.where(row8 == 2, x3, 0.0)))
            if hh == 0:
                ka.append(jnp.concatenate([k_t[:HEAD_DIM], fp, pad], axis=0).astype(BF16))
                va.append(jnp.where(lane == HEAD_DIM, 1.0, v).astype(BF16))
            else:
                ka.append(jnp.concatenate([fp, pad, k_t[HEAD_DIM:]], axis=0).astype(BF16))
                va.append(jnp.where(lane == 0, 1.0, v).astype(BF16))
        k_aug.append(ka)
        v_aug.append(va)

    for i in range(seq // tq):
        q = q_ref[i * tq:(i + 1) * tq, :].astype(F32)
        qs = (jnp.where(first, q, jnp.where(lane < HEAD_DIM + 3, -1.0, 0.0)).astype(BF16),
              jnp.where(first, jnp.where(lane < 3, -1.0, 0.0), q).astype(BF16))
        accs = []
        for hh in range(2):
            m = jnp.full((tq, 1), NEG, F32)
            acc = jnp.zeros((tq, LANES), F32)
            for j in range(i + 1):
                s = _dot(qs[hh], k_aug[j][hh])
                if j == i:
                    s = jnp.where(causal, s, NEG)
                m_new = jnp.maximum(m, jnp.max(s, axis=1, keepdims=True))
                p = jnp.exp2(s - m_new).astype(BF16)
                acc = jnp.exp2(m - m_new) * acc + _dot(p, v_aug[j][hh])
                m = m_new
            accs.append(acc)
        a0, a1 = accs
        o_ref[i * tq:(i + 1) * tq, :] = jnp.where(
            first, a0 / a0[:, HEAD_DIM:HEAD_DIM + 1], a1 / a1[:, 0:1]).astype(BF16)


def _flash(q, k_all, v_nat, fcum, layer, tq=512):
    b, s, wb = q.shape
    nh = fcum.shape[1]
    npp, hd, page = k_all.shape[1], k_all.shape[4], k_all.shape[5]
    f4 = fcum.reshape(b, nh, 1, s)
    full = pl.BlockSpec((None, s, LANES), lambda bi, hp: (bi, 0, hp))
    pair = pl.BlockSpec((None, npp, None, 2, hd, page), lambda bi, hp: (bi, 0, layer, hp, 0, 0))
    return pl.pallas_call(
        functools.partial(_flash_kernel, tq=tq),
        grid=(b, wb // LANES),
        in_specs=[full, pair, full, pl.BlockSpec((None, 2, 1, s), lambda bi, hp: (bi, hp, 0, 0))],
        out_specs=full,
        out_shape=jax.ShapeDtypeStruct((b, s, wb), BF16),
        compiler_params=_params(("arbitrary", "arbitrary")),
        name="fox_prompt",
    )(q, k_all, v_nat, f4)


def _paged_kernel(pt_ref, q_ref, kn_ref, vn_ref, lf_ref, ck_ref, cv_ref, cf_ref, o_ref,
                  kbuf, vbuf, fbuf, ksem, vsem, fsem, *, layer):
    n_pages = pt_ref.shape[1]
    n_heads, hd, page = kbuf.shape[2:]
    t_new, wb = q_ref.shape
    rows = t_new * n_heads
    b = pl.program_id(0)
    slot = lax.rem(b, 2)

    def copies(bb, sl):
        out = []
        for p in range(n_pages):
            pg = pt_ref[bb, p]
            out.append(pltpu.make_async_copy(ck_ref.at[pg, layer], kbuf.at[sl, p], ksem.at[sl]))
            out.append(pltpu.make_async_copy(cv_ref.at[pg, layer], vbuf.at[sl, p], vsem.at[sl]))
            out.append(pltpu.make_async_copy(cf_ref.at[pg, layer], fbuf.at[sl, p], fsem.at[sl]))
        return out

    @pl.when(b == 0)
    def _():
        for cp in copies(0, 0):
            cp.start()

    @pl.when(b + 1 < pl.num_programs(0))
    def _():
        for cp in copies(b + 1, 1 - slot):
            cp.start()

    for cp in copies(b, slot):
        cp.wait()

    f = fbuf[slot].reshape(n_pages * n_heads, page)
    lane = lax.broadcasted_iota(jnp.int32, f.shape, 1)
    incl = f
    sh = 1
    while sh < page:
        incl = incl + jnp.where(lane + sh < page, pltpu.roll(incl, page - sh, axis=1), 0.0)
        sh *= 2
    tot = jnp.broadcast_to(incl[:, :1], f.shape)
    nr = f.shape[0]
    rr = lax.broadcasted_iota(jnp.int32, (nr, nr), 0)
    cc = lax.broadcasted_iota(jnp.int32, (nr, nr), 1)
    later = (_mod(rr, n_heads) == _mod(cc, n_heads)) & (cc > rr)
    su = jnp.where(later, 1.0, 0.0).astype(BF16)
    t1, t2, t3 = _split3(tot)
    r_all = incl - f + _dot(su, t1) + _dot(su, t2) + _dot(su, t3)

    own = (lax.broadcasted_iota(jnp.int32, (n_heads, wb), 0) ==
           _div(lax.broadcasted_iota(jnp.int32, (n_heads, wb), 1), hd))
    q = q_ref[...]
    qbd = jnp.concatenate(
        [jnp.where(own, jnp.broadcast_to(q[t:t + 1, :], (n_heads, wb)), 0.0) for t in range(t_new)],
        axis=0).astype(BF16)

    s_past = []
    for p in range(n_pages):
        k_t = kbuf[slot, p].reshape(wb, page).astype(BF16)
        r8 = r_all[p * n_heads:(p + 1) * n_heads, :]
        s_past.append(_dot(qbd, k_t) + jnp.concatenate([r8] * t_new, axis=0))

    g = lf_ref[...]
    ln = lax.broadcasted_iota(jnp.int32, g.shape, 1)
    sh = 1
    while sh < t_new:
        g = g + jnp.where(ln >= sh, pltpu.roll(g, sh, axis=1), 0.0)
        sh *= 2
    n_pad = kn_ref.shape[0]
    g = jnp.concatenate([g[:, :n_pad]] * t_new, axis=0)
    s_new = _dot_nt(qbd, kn_ref[...].astype(BF16)) - g
    rn = lax.broadcasted_iota(jnp.int32, (rows, n_pad), 0)
    cn = lax.broadcasted_iota(jnp.int32, (rows, n_pad), 1)
    s_new = jnp.where(cn <= _div(rn, n_heads), s_new, NEG)

    m = jnp.max(s_new, axis=1, keepdims=True)
    m_el = s_past[0]
    for s in s_past[1:]:
        m_el = jnp.maximum(m_el, s)
    m = jnp.maximum(m, jnp.max(m_el, axis=1, keepdims=True))
    pr = jnp.exp(s_new - m)
    l = jnp.sum(pr, axis=1, keepdims=True)
    acc = _dot(pr.astype(BF16), vn_ref[...].astype(BF16))
    l_el = None
    for p in range(n_pages):
        pr = jnp.exp(s_past[p] - m)
        l_el = pr if l_el is None else l_el + pr
        acc = acc + _dot_nt(pr.astype(BF16), vbuf[slot, p].reshape(wb, page).astype(BF16))
    l = l + jnp.sum(l_el, axis=1, keepdims=True)

    own_rows = jnp.concatenate([own] * t_new, axis=0)
    o = jnp.where(own_rows, acc / l, 0.0)
    o_ref[...] = jnp.sum(o.reshape(t_new, n_heads, wb), axis=1)


def _paged(page_table, q, kn, vn, lf, cache_k, cache_v, cache_f, layer):
    db, t_new, wb = q.shape
    n_pages = page_table.shape[1]
    n_heads, hd, page = cache_k.shape[2:]
    per = lambda shape: pl.BlockSpec((None,) + shape, lambda b, pt: (b, 0, 0))
    any_spec = pl.BlockSpec(memory_space=pl.ANY)
    grid_spec = pltpu.PrefetchScalarGridSpec(
        num_scalar_prefetch=1,
        grid=(db,),
        in_specs=[per((t_new, wb)), per(kn.shape[1:]), per(vn.shape[1:]), per(lf.shape[1:]),
                  any_spec, any_spec, any_spec],
        out_specs=per((t_new, wb)),
        scratch_shapes=[
            pltpu.VMEM((2, n_pages, n_heads, hd, page), F32),
            pltpu.VMEM((2, n_pages, n_heads, hd, page), F32),
            pltpu.VMEM((2, n_pages, n_heads, page), F32),
            pltpu.SemaphoreType.DMA((2,)),
            pltpu.SemaphoreType.DMA((2,)),
            pltpu.SemaphoreType.DMA((2,)),
        ],
    )
    return pl.pallas_call(
        functools.partial(_paged_kernel, layer=layer),
        grid_spec=grid_spec,
        out_shape=jax.ShapeDtypeStruct((db, t_new, wb), F32),
        compiler_params=_params(("arbitrary",)),
        name="fox_sample",
    )(page_table, q, kn, vn, lf, cache_k, cache_v, cache_f)


def _tail_kernel(a_ref, o_ref, x_ref, gate1_ref, shift_ref, scale_ref, gate2_ref,
                 gpm_ref, gpf_ref, gqf_ref, woa_ref, woo_ref, wgu_ref, wdn_ref, y_ref, *, ff_chunk):
    d_ff = wdn_ref.shape[0]
    y = _dot(a_ref[...].astype(BF16), woa_ref[...]) + _dot(o_ref[...].astype(BF16), woo_ref[...])
    x1 = x_ref[...] + gate1_ref[...] * _rms(y, gpm_ref[...])
    h = (_rms(x1, gpf_ref[...]) * (1.0 + scale_ref[...]) + shift_ref[...]).astype(BF16)
    f = None
    for c0 in range(0, d_ff, ff_chunk):
        gte = _dot(h, wgu_ref[:, c0:c0 + ff_chunk])
        up = _dot(h, wgu_ref[:, d_ff + c0:d_ff + c0 + ff_chunk])
        t = (gte * jax.nn.sigmoid(gte) * up).astype(BF16)
        part = _dot(t, wdn_ref[c0:c0 + ff_chunk, :])
        f = part if f is None else f + part
    y_ref[...] = x1 + gate2_ref[...] * _rms(f, gqf_ref[...])


def _ff_chunk(d_ff):
    for c in (512, 256, 128):
        if d_ff % c == 0:
            return c
    return d_ff


def _tail_prompt(a, o, x, mods, g_post_mix, g_pre_ffn, g_post_ffn, woa, woo, wgu, wdn, layer, tm=512):
    b, s, d = x.shape
    row = lambda width: pl.BlockSpec((None, tm, width), lambda bi, i: (bi, i, 0))
    mod = lambda j: pl.BlockSpec((None, None, 1, d), lambda bi, i: (layer, bi, 0, j))
    stacked = [g_post_mix, g_pre_ffn, g_post_ffn, woa, woo, wgu, wdn]
    return pl.pallas_call(
        functools.partial(_tail_kernel, ff_chunk=_ff_chunk(wdn.shape[1])),
        grid=(b, s // tm),
        in_specs=[row(a.shape[2]), row(o.shape[2]), row(d), mod(2), mod(3), mod(4), mod(5)] +
                 [_layer_spec(t, layer) for t in stacked],
        out_specs=row(d),
        out_shape=jax.ShapeDtypeStruct((b, s, d), F32),
        compiler_params=_params(("arbitrary", "arbitrary")),
        name="tail_prompt",
    )(a, o, x, mods, mods, mods, mods, *stacked)


def _tail_sample(a, o, x, mods, g_post_mix, g_pre_ffn, g_post_ffn, woa, woo, wgu, wdn, layer):
    n, d = x.shape
    tm = mods.shape[1]
    row = lambda width: pl.BlockSpec((tm, width), lambda i: (i, 0))
    mod = lambda j: pl.BlockSpec((None, tm, d), lambda i: (layer, 0, j))
    stacked = [g_post_mix, g_pre_ffn, g_post_ffn, woa, woo, wgu, wdn]
    return pl.pallas_call(
        functools.partial(_tail_kernel, ff_chunk=_ff_chunk(wdn.shape[1])),
        grid=(n // tm,),
        in_specs=[row(a.shape[1]), row(o.shape[1]), row(d), mod(2), mod(3), mod(4), mod(5)] +
                 [_layer_spec(t, layer) for t in stacked],
        out_specs=row(d),
        out_shape=jax.ShapeDtypeStruct((n, d), F32),
        compiler_params=_params(("arbitrary",)),
        name="tail_sample",
    )(a, o, x, mods, mods, mods, mods, *stacked)


def kernel(x_prompt, x_sample, c_prompt, c_sample, cache_k, cache_v, cache_logf, page_table, w_ada, b_ada, g_pre_mix, g_post_mix, g_pre_ffn, g_post_ffn, w_in, b_f, g_sgu, w_s, b_s, w_out, w_gu, w_down):
    bsz, seq, d = x_prompt.shape
    db, t_new, _ = x_sample.shape
    depth = w_ada.shape[0]
    _, _, page, hb, hd = cache_k.shape
    ha = w_s.shape[1]
    wa, wb = ha * hd, hb * hd
    assert hd == HEAD_DIM and page == CHUNK and w_s.shape[2] == CHUNK and 2 * hd == LANES
    npp = seq // page

    mods = _ada(jnp.concatenate([c_prompt, c_sample], axis=0), w_ada, b_ada)
    mods_p = mods[:, :bsz].reshape(depth, bsz, 1, 6 * d)
    mods_s = mods[:, bsz:]

    o_main = 2 * wa + 3 * wb
    o_k = 2 * wa + wb
    w_main = w_in[:, :, :o_main].astype(BF16)
    w_uvqv = jnp.concatenate([w_main[:, :, :o_k], w_main[:, :, o_k + wb:]], axis=2)
    w_kv_t = jnp.swapaxes(w_main[:, :, o_k:], 1, 2)
    wf_t = jnp.swapaxes(w_in[:, :, o_main:], 1, 2)
    wf_t = jnp.pad(wf_t, ((0, 0), (0, BF16_ROWS - hb), (0, 0))).astype(BF16)
    woa = w_out[:, :wa].astype(BF16)
    woo = w_out[:, wa:].astype(BF16)
    wgu = w_gu.astype(BF16)
    wdn = w_down.astype(BF16)
    vec = lambda g: g.reshape(depth, 1, -1)
    g_pre_mix, g_post_mix, g_pre_ffn, g_post_ffn, g_sgu = map(
        vec, (g_pre_mix, g_post_mix, g_pre_ffn, g_post_ffn, g_sgu))
    b_f = b_f.reshape(depth, hb, 1)
    grp = jnp.arange(wa) // hd
    ones_bd = (grp[:, None] == grp[None, :]).astype(BF16)
    bias_exp = jnp.repeat(jnp.swapaxes(b_s, 1, 2), hd, axis=2)
    wvec = jnp.repeat(jnp.transpose(w_s[:, :, :t_new, :t_new], (0, 2, 3, 1)), hd, axis=3)
    wvec = wvec.reshape(depth, t_new * t_new, wa)
    bvec = bias_exp[:, :t_new]
    ck_t = jnp.transpose(cache_k, (0, 1, 3, 4, 2))
    cv_t = jnp.transpose(cache_v, (0, 1, 3, 4, 2))
    cf_t = jnp.transpose(cache_logf, (0, 1, 3, 2))

    xp = x_prompt
    xs = jnp.swapaxes(x_sample, 0, 1).reshape(t_new * db, d)
    k_all = jnp.zeros((bsz, npp, depth, hb, hd, page), F32)
    v_all = jnp.zeros((bsz, npp, depth, hb, hd, page), F32)
    pf, sk, sv, sf, sa = [], [], [], [], []
    per_sample = lambda t: jnp.swapaxes(t.reshape(t_new, db, -1), 0, 1)
    pad_rows = lambda t: jnp.pad(t, ((0, 0), (0, BF16_ROWS - t_new), (0, 0)))
    for l in range(depth):
        a, q, v_nat, k_all, v_all, lf, fcum = _mix_in_prompt(
            xp, mods_p, g_pre_mix, w_uvqv, w_kv_t, wf_t, b_f, g_sgu, ones_bd, w_s, bias_exp,
            k_all, v_all, l)
        o = _flash(q, k_all, v_nat, fcum, l)
        xp = _tail_prompt(a, o, xp, mods_p, g_post_mix, g_pre_ffn, g_post_ffn, woa, woo, wgu, wdn, l)
        pf.append(lf)

        u, va, q, k, v, lf = _mix_in_sample(xs, mods_s, g_pre_mix, w_main, wf_t, b_f, g_sgu,
                                            ones_bd, l)
        a = _sgu_sample(u.reshape(t_new, db, wa), va.reshape(t_new, db, wa), wvec[l], bvec[l])
        k3, v3 = per_sample(k), per_sample(v)
        lf3 = jnp.transpose(lf.reshape(hb, t_new, db), (2, 0, 1))
        lf_pad = jnp.pad(lf3, ((0, 0), (0, 0), (0, LANES - t_new)))
        o = _paged(page_table, per_sample(q), pad_rows(k3), pad_rows(v3), lf_pad,
                   ck_t, cv_t, cf_t, l)
        xs = _tail_sample(a.reshape(t_new * db, wa), jnp.swapaxes(o, 0, 1).reshape(t_new * db, wb),
                          xs, mods_s, g_post_mix, g_pre_ffn, g_post_ffn, woa, woo, wgu, wdn, l)
        sk.append(k3.reshape(db, t_new, hb, hd))
        sv.append(v3.reshape(db, t_new, hb, hd))
        sf.append(jnp.swapaxes(lf3, 1, 2))
        sa.append(per_sample(va).reshape(db, t_new, ha, hd))

    prompt_k = jnp.transpose(k_all, (0, 1, 2, 5, 3, 4))
    prompt_v = jnp.transpose(v_all, (0, 1, 2, 5, 3, 4))
    prompt_logf = jnp.stack(
        [jnp.swapaxes(t, 1, 2).reshape(bsz, npp, page, hb) for t in pf], axis=2)
    y_sample = jnp.swapaxes(xs.reshape(t_new, db, d), 0, 1)
    return (xp, y_sample, prompt_k, prompt_v, prompt_logf, jnp.stack(sk, axis=1),
            jnp.stack(sv, axis=1), jnp.stack(sf, axis=1), jnp.stack(sa, axis=1))
```

```python
import functools

import jax
import jax.numpy as jnp
from jax import lax
from jax.experimental import pallas as pl
from jax.experimental.pallas import tpu as pltpu

F32 = jnp.float32
BF16 = jnp.bfloat16

HEAD_DIM = 64
CHUNK = 128
EPS = 1e-6
LANES = 128
SUBLANES = 8
BF16_ROWS = 16
NEG = -1e30
LOG2E = 1.4426950408889634
VMEM_LIMIT = 56 * 1024 * 1024

_NT = (((1,), (1,)), ((), ()))


def _params(sem, vmem=VMEM_LIMIT):
    return pltpu.CompilerParams(dimension_semantics=sem, vmem_limit_bytes=vmem)


def _rms(x, g):
    return x * lax.rsqrt(jnp.mean(x * x, axis=-1, keepdims=True) + EPS) * g


def _dot(a, b):
    return jnp.dot(a, b, preferred_element_type=F32)


def _dot_nt(a, b):
    return lax.dot_general(a, b, _NT, preferred_element_type=F32)


def _mod(x, n):
    return jnp.bitwise_and(x, n - 1) if n & (n - 1) == 0 else x % n


def _div(x, n):
    return lax.shift_right_logical(x, n.bit_length() - 1) if n & (n - 1) == 0 else x // n


def _split3(x):
    x1 = x.astype(BF16)
    r1 = x - x1.astype(F32)
    x2 = r1.astype(BF16)
    x3 = (r1 - x2.astype(F32)).astype(BF16)
    return x1, x2, x3


def _log_sigmoid(x):
    return jnp.minimum(x, 0.0) - jnp.log1p(jnp.exp(-jnp.abs(x)))


def _layer_spec(arr, layer):
    nd = arr.ndim - 1
    return pl.BlockSpec((None,) + arr.shape[1:], lambda *_: (layer,) + (0,) * nd,
                        pipeline_mode=pl.Buffered(1))


def _const_spec(shape):
    nd = len(shape)
    return pl.BlockSpec(shape, lambda *_: (0,) * nd, pipeline_mode=pl.Buffered(1))


def _ada_kernel(c_ref, w_ref, b_ref, o_ref):
    c = c_ref[...]
    s = (c * jax.nn.sigmoid(c)).astype(BF16)
    o_ref[...] = _dot(s, w_ref[...].astype(BF16)) + b_ref[...]


def _ada(c_all, w_ada, b_ada, tn=1024):
    depth, d, e = w_ada.shape
    n = c_all.shape[0]
    return pl.pallas_call(
        _ada_kernel,
        grid=(depth, e // tn),
        in_specs=[
            pl.BlockSpec((n, d), lambda l, j: (0, 0)),
            pl.BlockSpec((None, d, tn), lambda l, j: (l, 0, j)),
            pl.BlockSpec((None, 1, tn), lambda l, j: (l, 0, j)),
        ],
        out_specs=pl.BlockSpec((None, n, tn), lambda l, j: (l, 0, j)),
        out_shape=jax.ShapeDtypeStruct((depth, n, e), F32),
        compiler_params=_params(("arbitrary", "arbitrary")),
        name="ada",
    )(c_all, w_ada, b_ada.reshape(depth, 1, e))


def _norm_mod(x_ref, shift_ref, scale_ref, g_ref):
    h = _rms(x_ref[...], g_ref[...]) * (1.0 + scale_ref[...]) + shift_ref[...]
    return h.astype(BF16)


def _gates(z, wa, gsgu_ref, ones_ref):
    u = jax.nn.gelu(z[:, :wa])
    va = jax.nn.gelu(z[:, wa:2 * wa])
    gs = _dot((va * va).astype(BF16), ones_ref[...])
    va = va * lax.rsqrt(gs * (1.0 / HEAD_DIM) + EPS) * gsgu_ref[...]
    return u, va


def _logf_t(wf_ref, bf_ref, hb):
    nh = bf_ref.shape[0]
    return _log_sigmoid(_dot_nt(wf_ref[...], hb)[:nh] + bf_ref[...])


def _mix_in_prompt_kernel(x_ref, shift_ref, scale_ref, g_ref, w_ref, wkv_ref, wf_ref, bf_ref,
                          gsgu_ref, ones_ref, ws_ref, bias_ref, kin_ref, vin_ref,
                          a_ref, q_ref, vn_ref, k_ref, v_ref, lf_ref, fc_ref, carry_ref):
    del kin_ref, vin_ref
    tm = x_ref.shape[0]
    wa = ones_ref.shape[0]
    hb = _norm_mod(x_ref, shift_ref, scale_ref, g_ref)
    z = _dot(hb, w_ref[...])
    u, va = _gates(z, wa, gsgu_ref, ones_ref)
    wb = (z.shape[1] - 2 * wa) // 2
    q_ref[...] = (z[:, 2 * wa:2 * wa + wb] * (HEAD_DIM ** -0.5 * LOG2E)).astype(BF16)
    vn_ref[...] = z[:, 2 * wa + wb:].astype(BF16)

    kv_t = _dot_nt(wkv_ref[...], hb)
    for pg in range(tm // CHUNK):
        cols = slice(pg * CHUNK, (pg + 1) * CHUNK)
        k_ref[pg] = kv_t[:wb, cols].reshape(k_ref.shape[1:])
        v_ref[pg] = kv_t[wb:, cols].reshape(v_ref.shape[1:])

    logf_t = _logf_t(wf_ref, bf_ref, hb)
    lf_ref[...] = logf_t

    @pl.when(pl.program_id(1) == 0)
    def _():
        carry_ref[...] = jnp.zeros_like(carry_ref)

    r = lax.broadcasted_iota(jnp.int32, (tm, tm), 0)
    c = lax.broadcasted_iota(jnp.int32, (tm, tm), 1)
    tri = jnp.where(r <= c, 1.0, 0.0).astype(BF16)
    l1, l2, l3 = _split3(logf_t)
    fc = _dot(l1, tri) + _dot(l2, tri) + _dot(l3, tri) + carry_ref[:, :1]
    fc_ref[...] = fc
    carry_ref[...] = jnp.broadcast_to(fc[:, tm - 1:tm], carry_ref.shape)

    n_heads = ws_ref.shape[0]
    rr = lax.broadcasted_iota(jnp.int32, (CHUNK, CHUNK), 0)
    cc = lax.broadcasted_iota(jnp.int32, (CHUNK, CHUNK), 1)
    low = cc <= rr
    first = lax.broadcasted_iota(jnp.int32, (CHUNK, LANES), 1) < HEAD_DIM
    wts = [jnp.where(low, ws_ref[hh], 0.0).astype(BF16) for hh in range(n_heads)]
    for ci in range(tm // CHUNK):
        rows = slice(ci * CHUNK, (ci + 1) * CHUNK)
        for p in range(n_heads // 2):
            cols = slice(p * LANES, (p + 1) * LANES)
            vp = va[rows, cols].astype(BF16)
            mixed = jnp.where(first, _dot(wts[2 * p], vp), _dot(wts[2 * p + 1], vp))
            a_ref[rows, cols] = (u[rows, cols] * (mixed + bias_ref[:, cols])).astype(BF16)


def _mix_in_prompt(x, mods, g_pre, w_uvqv, w_kv_t, wf_t, b_f, g_sgu, ones_bd, w_s, bias_exp,
                   k_all, v_all, layer, tm=512):
    b, s, d = x.shape
    wa = ones_bd.shape[0]
    wb = (w_uvqv.shape[2] - 2 * wa) // 2
    nh = b_f.shape[1]
    ppt = tm // CHUNK
    row = lambda width: pl.BlockSpec((None, tm, width), lambda bi, i: (bi, i, 0))
    mod = lambda j: pl.BlockSpec((None, None, 1, d), lambda bi, i: (layer, bi, 0, j))
    lane_major = pl.BlockSpec((None, nh, tm), lambda bi, i: (bi, 0, i))
    kv_out = pl.BlockSpec((None, ppt, None) + k_all.shape[3:],
                          lambda bi, i: (bi, i, layer, 0, 0, 0))
    any_spec = pl.BlockSpec(memory_space=pl.ANY)
    stacked = [g_pre, w_uvqv, w_kv_t, wf_t, b_f, g_sgu]
    ins = [x, mods, mods] + stacked + [ones_bd, w_s, bias_exp, k_all, v_all]
    return pl.pallas_call(
        _mix_in_prompt_kernel,
        grid=(b, s // tm),
        in_specs=[row(d), mod(0), mod(1)] + [_layer_spec(t, layer) for t in stacked] +
                 [_const_spec(ones_bd.shape), _layer_spec(w_s, layer), _layer_spec(bias_exp, layer),
                  any_spec, any_spec],
        out_specs=[row(wa), row(wb), row(wb), kv_out, kv_out, lane_major, lane_major],
        out_shape=[jax.ShapeDtypeStruct((b, s, wa), BF16), jax.ShapeDtypeStruct((b, s, wb), BF16),
                   jax.ShapeDtypeStruct((b, s, wb), BF16),
                   jax.ShapeDtypeStruct(k_all.shape, F32), jax.ShapeDtypeStruct(v_all.shape, F32),
                   jax.ShapeDtypeStruct((b, nh, s), F32), jax.ShapeDtypeStruct((b, nh, s), F32)],
        scratch_shapes=[pltpu.VMEM((nh, LANES), F32)],
        input_output_aliases={len(ins) - 2: 3, len(ins) - 1: 4},
        compiler_params=_params(("arbitrary", "arbitrary")),
        name="mix_in_prompt",
    )(*ins)


def _mix_in_sample_kernel(x_ref, shift_ref, scale_ref, g_ref, w_ref, wf_ref, bf_ref, gsgu_ref,
                          ones_ref, u_ref, va_ref, q_ref, k_ref, v_ref, lf_ref):
    wa = ones_ref.shape[0]
    hb = _norm_mod(x_ref, shift_ref, scale_ref, g_ref)
    z = _dot(hb, w_ref[...])
    u, va = _gates(z, wa, gsgu_ref, ones_ref)
    u_ref[...] = u
    va_ref[...] = va
    o = 2 * wa
    wb = (z.shape[1] - o) // 3
    q_ref[...] = z[:, o:o + wb] * (HEAD_DIM ** -0.5)
    k_ref[...] = z[:, o + wb:o + 2 * wb]
    v_ref[...] = z[:, o + 2 * wb:o + 3 * wb]
    lf_ref[...] = _logf_t(wf_ref, bf_ref, hb)


def _mix_in_sample(x, mods, g_pre, w_main, wf_t, b_f, g_sgu, ones_bd, layer):
    n, d = x.shape
    tm = mods.shape[1]
    wa = ones_bd.shape[0]
    wb = (w_main.shape[2] - 2 * wa) // 3
    nh = b_f.shape[1]
    row = lambda width: pl.BlockSpec((tm, width), lambda i: (i, 0))
    mod = lambda j: pl.BlockSpec((None, tm, d), lambda i: (layer, 0, j))
    stacked = [g_pre, w_main, wf_t, b_f, g_sgu]
    return pl.pallas_call(
        _mix_in_sample_kernel,
        grid=(n // tm,),
        in_specs=[row(d), mod(0), mod(1)] + [_layer_spec(t, layer) for t in stacked] +
                 [_const_spec(ones_bd.shape)],
        out_specs=[row(wa), row(wa), row(wb), row(wb), row(wb),
                   pl.BlockSpec((nh, tm), lambda i: (0, i))],
        out_shape=[jax.ShapeDtypeStruct((n, wa), F32), jax.ShapeDtypeStruct((n, wa), F32),
                   jax.ShapeDtypeStruct((n, wb), F32), jax.ShapeDtypeStruct((n, wb), F32),
                   jax.ShapeDtypeStruct((n, wb), F32), jax.ShapeDtypeStruct((nh, n), F32)],
        compiler_params=_params(("arbitrary",)),
        name="mix_in_sample",
    )(x, mods, mods, *stacked, ones_bd)


def _sgu_sample_kernel(u_ref, va_ref, w_ref, b_ref, a_ref):
    t_new = u_ref.shape[0]
    for t in range(t_new):
        mixed = b_ref[t:t + 1, :]
        for j in range(t + 1):
            mixed = mixed + w_ref[t * t_new + j:t * t_new + j + 1, :] * va_ref[j]
        a_ref[t] = (u_ref[t] * mixed).astype(BF16)


def _sgu_sample(u, va, wvec, bvec):
    return pl.pallas_call(
        _sgu_sample_kernel,
        out_shape=jax.ShapeDtypeStruct(u.shape, BF16),
        name="sgu_sample",
    )(u, va, wvec, bvec)


def _pieces3(x):
    x1 = x.astype(BF16).astype(F32)
    r1 = x - x1
    x2 = r1.astype(BF16).astype(F32)
    x3 = (r1 - x2).astype(BF16).astype(F32)
    return x1, x2, x3


def _flash_key_block(k_ref, v_ref, f_ref, j, tq):
    ppb = tq // CHUNK
    lane = lax.broadcasted_iota(jnp.int32, (tq, LANES), 1)
    row8 = lax.broadcasted_iota(jnp.int32, (SUBLANES, tq), 0)
    pad = jnp.zeros((HEAD_DIM - SUBLANES, tq), F32)
    k_t = jnp.concatenate([k_ref[j * ppb + p].reshape(LANES, CHUNK) for p in range(ppb)], axis=1)
    v = v_ref[j * tq:(j + 1) * tq, :].astype(F32)
    ka, va = [], []
    for hh in range(2):
        x1, x2, x3 = _pieces3(f_ref[hh, :, j * tq:(j + 1) * tq] * LOG2E)
        fp = jnp.where(row8 == 0, x1, jnp.where(row8 == 1, x2, jnp.where(row8 == 2, x3, 0.0)))
        if hh == 0:
            ka.append(jnp.concatenate([k_t[:HEAD_DIM], fp, pad], axis=0).astype(BF16))
            va.append(jnp.where(lane == HEAD_DIM, 1.0, v).astype(BF16))
        else:
            ka.append(jnp.concatenate([fp, pad, k_t[HEAD_DIM:]], axis=0).astype(BF16))
            va.append(jnp.where(lane == 0, 1.0, v).astype(BF16))
    return ka, va


def _flash_query_block(q_ref, o_ref, k_aug, v_aug, i, tq):
    lane = lax.broadcasted_iota(jnp.int32, (tq, LANES), 1)
    first = lane < HEAD_DIM
    causal = (lax.broadcasted_iota(jnp.int32, (tq, tq), 1) <=
              lax.broadcasted_iota(jnp.int32, (tq, tq), 0))
    q = q_ref[i * tq:(i + 1) * tq, :].astype(F32)
    qs = (jnp.where(first, q, jnp.where(lane < HEAD_DIM + 3, -1.0, 0.0)).astype(BF16),
          jnp.where(first, jnp.where(lane < 3, -1.0, 0.0), q).astype(BF16))
    accs = []
    for hh in range(2):
        m = jnp.full((tq, 1), NEG, F32)
        acc = jnp.zeros((tq, LANES), F32)
        for j in range(i + 1):
            s = _dot(qs[hh], k_aug[j][hh])
            if j == i:
                s = jnp.where(causal, s, NEG)
            m_new = jnp.maximum(m, jnp.max(s, axis=1, keepdims=True))
            p = jnp.exp2(s - m_new).astype(BF16)
            acc = jnp.exp2(m - m_new) * acc + _dot(p, v_aug[j][hh])
            m = m_new
        accs.append(acc)
    a0, a1 = accs
    o_ref[i * tq:(i + 1) * tq, :] = jnp.where(
        first, a0 / a0[:, HEAD_DIM:HEAD_DIM + 1], a1 / a1[:, 0:1]).astype(BF16)


def _paged_sample(q, kn, vn, lf, f, k_page, v_page, n_pages, n_heads):
    t_new, wb = q.shape
    hd = wb // n_heads
    page = f.shape[1]
    rows = t_new * n_heads

    after = jnp.where(lax.broadcasted_iota(jnp.int32, (page, page), 0) >
                      lax.broadcasted_iota(jnp.int32, (page, page), 1), 1.0, 0.0).astype(BF16)
    ones = jnp.ones((page, page), BF16)
    f1, f2, f3 = _split3(f)
    tot = _dot(f1, ones) + _dot(f2, ones) + _dot(f3, ones)
    nr = f.shape[0]
    rr = lax.broadcasted_iota(jnp.int32, (nr, nr), 0)
    cc = lax.broadcasted_iota(jnp.int32, (nr, nr), 1)
    later = (_mod(rr, n_heads) == _mod(cc, n_heads)) & (cc > rr)
    su = jnp.where(later, 1.0, 0.0).astype(BF16)
    t1, t2, t3 = _split3(tot)
    r_all = (_dot(f1, after) + _dot(f2, after) + _dot(f3, after) +
             _dot(su, t1) + _dot(su, t2) + _dot(su, t3))

    own = (lax.broadcasted_iota(jnp.int32, (n_heads, wb), 0) ==
           _div(lax.broadcasted_iota(jnp.int32, (n_heads, wb), 1), hd))
    qbd = jnp.concatenate(
        [jnp.where(own, jnp.broadcast_to(q[t:t + 1, :], (n_heads, wb)), 0.0) for t in range(t_new)],
        axis=0).astype(BF16)

    s_past = []
    for p in range(n_pages):
        r8 = r_all[p * n_heads:(p + 1) * n_heads, :]
        s_past.append(_dot(qbd, k_page(p)) + jnp.concatenate([r8] * t_new, axis=0))

    g = lf
    ln = lax.broadcasted_iota(jnp.int32, g.shape, 1)
    sh = 1
    while sh < t_new:
        g = g + jnp.where(ln >= sh, pltpu.roll(g, sh, axis=1), 0.0)
        sh *= 2
    n_pad = kn.shape[0]
    g = jnp.concatenate([g[:, :n_pad]] * t_new, axis=0)
    s_new = _dot_nt(qbd, kn.astype(BF16)) - g
    rn = lax.broadcasted_iota(jnp.int32, (rows, n_pad), 0)
    cn = lax.broadcasted_iota(jnp.int32, (rows, n_pad), 1)
    s_new = jnp.where(cn <= _div(rn, n_heads), s_new, NEG)

    m = jnp.max(s_new, axis=1, keepdims=True)
    m_el = s_past[0]
    for s in s_past[1:]:
        m_el = jnp.maximum(m_el, s)
    m = jnp.maximum(m, jnp.max(m_el, axis=1, keepdims=True))
    pr = jnp.exp(s_new - m)
    l = jnp.sum(pr, axis=1, keepdims=True)
    acc = _dot(pr.astype(BF16), vn.astype(BF16))
    l_el = None
    for p in range(n_pages):
        pr = jnp.exp(s_past[p] - m)
        l_el = pr if l_el is None else l_el + pr
        acc = acc + _dot_nt(pr.astype(BF16), v_page(p))
    l = l + jnp.sum(l_el, axis=1, keepdims=True)

    own_rows = jnp.concatenate([own] * t_new, axis=0)
    o = jnp.where(own_rows, acc / l, 0.0)
    return jnp.sum(o.reshape(t_new, n_heads, wb), axis=1)


def _fox_kernel(pt_ref, q_ref, k_ref, v_ref, f_ref, qs_ref, kn_ref, vn_ref, lf_ref,
                ck_ref, cv_ref, cf_ref, o_ref, os_ref, kbuf, vbuf, fbuf, ksem, vsem, fsem,
                *, layer, tq):
    n_pages = pt_ref.shape[1]
    n_heads, hd, page = kbuf.shape[2:]
    wb = n_heads * hd
    spp = qs_ref.shape[0]
    n_q = q_ref.shape[0] // tq
    step = pl.program_id(0) * pl.num_programs(1) + pl.program_id(1)
    n_samples = pl.num_programs(0) * pl.num_programs(1) * spp

    def copies(n, sl):
        out = []
        for p in range(n_pages):
            pg = pt_ref[n, p]
            out.append(pltpu.make_async_copy(ck_ref.at[pg, layer], kbuf.at[sl, p], ksem.at[sl]))
            out.append(pltpu.make_async_copy(cv_ref.at[pg, layer], vbuf.at[sl, p], vsem.at[sl]))
            out.append(pltpu.make_async_copy(cf_ref.at[pg, layer], fbuf.at[sl, p], fsem.at[sl]))
        return out

    @pl.when(step == 0)
    def _():
        for cp in copies(0, 0):
            cp.start()

    k_aug, v_aug = [], []
    for s in range(spp):
        n = step * spp + s
        sl = s % 2

        @pl.when(n + 1 < n_samples)
        def _(n=n, sl=sl):
            for cp in copies(n + 1, 1 - sl):
                cp.start()

        for cp in copies(n, sl):
            cp.wait()
        os_ref[s] = _paged_sample(
            qs_ref[s], kn_ref[s], vn_ref[s], lf_ref[s], fbuf[sl].reshape(n_pages * n_heads, page),
            lambda p, sl=sl: kbuf[sl, p].reshape(wb, page).astype(BF16),
            lambda p, sl=sl: vbuf[sl, p].reshape(wb, page).astype(BF16), n_pages, n_heads)
        for i in range(n_q):
            if (i * spp) // n_q == s:
                ka, va = _flash_key_block(k_ref, v_ref, f_ref, i, tq)
                k_aug.append(ka)
                v_aug.append(va)
                _flash_query_block(q_ref, o_ref, k_aug, v_aug, i, tq)


def _fox(page_table, q, k_all, v_nat, fcum, qs, kn, vn, lf, cache_k, cache_v, cache_f, layer,
         tq=512):
    b, s, wb = q.shape
    nh = fcum.shape[1]
    npp, hd, page = k_all.shape[1], k_all.shape[4], k_all.shape[5]
    db, t_new, _ = qs.shape
    n_pages = page_table.shape[1]
    n_pairs = wb // LANES
    spp = db // (b * n_pairs)
    assert spp * b * n_pairs == db and spp % 2 == 0 and s % tq == 0
    f4 = fcum.reshape(b, nh, 1, s)
    full = pl.BlockSpec((None, s, LANES), lambda bi, hp, pt: (bi, 0, hp))
    pair = pl.BlockSpec((None, npp, None, 2, hd, page),
                        lambda bi, hp, pt: (bi, 0, layer, hp, 0, 0))
    per = lambda t: pl.BlockSpec((spp,) + t.shape[1:],
                                 lambda bi, hp, pt: (bi * n_pairs + hp, 0, 0))
    any_spec = pl.BlockSpec(memory_space=pl.ANY)
    grid_spec = pltpu.PrefetchScalarGridSpec(
        num_scalar_prefetch=1,
        grid=(b, n_pairs),
        in_specs=[full, pair, full,
                  pl.BlockSpec((None, 2, 1, s), lambda bi, hp, pt: (bi, hp, 0, 0)),
                  per(qs), per(kn), per(vn), per(lf), any_spec, any_spec, any_spec],
        out_specs=[full, per(qs)],
        scratch_shapes=[
            pltpu.VMEM((2, n_pages, nh, hd, page), F32),
            pltpu.VMEM((2, n_pages, nh, hd, page), F32),
            pltpu.VMEM((2, n_pages, nh, page), F32),
            pltpu.SemaphoreType.DMA((2,)),
            pltpu.SemaphoreType.DMA((2,)),
            pltpu.SemaphoreType.DMA((2,)),
        ],
    )
    return pl.pallas_call(
        functools.partial(_fox_kernel, layer=layer, tq=tq),
        grid_spec=grid_spec,
        out_shape=[jax.ShapeDtypeStruct((b, s, wb), BF16),
                   jax.ShapeDtypeStruct((db, t_new, wb), F32)],
        compiler_params=_params(("arbitrary", "arbitrary")),
        name="fox",
    )(page_table, q, k_all, v_nat, f4, qs, kn, vn, lf, cache_k, cache_v, cache_f)


def _tail_kernel(a_ref, o_ref, x_ref, gate1_ref, shift_ref, scale_ref, gate2_ref,
                 gpm_ref, gpf_ref, gqf_ref, woa_ref, woo_ref, wgu_ref, wdn_ref, y_ref, *, ff_chunk):
    d_ff = wdn_ref.shape[0]
    y = _dot(a_ref[...].astype(BF16), woa_ref[...]) + _dot(o_ref[...].astype(BF16), woo_ref[...])
    x1 = x_ref[...] + gate1_ref[...] * _rms(y, gpm_ref[...])
    h = (_rms(x1, gpf_ref[...]) * (1.0 + scale_ref[...]) + shift_ref[...]).astype(BF16)
    f = None
    for c0 in range(0, d_ff, ff_chunk):
        gte = _dot(h, wgu_ref[:, c0:c0 + ff_chunk])
        up = _dot(h, wgu_ref[:, d_ff + c0:d_ff + c0 + ff_chunk])
        t = (gte * jax.nn.sigmoid(gte) * up).astype(BF16)
        part = _dot(t, wdn_ref[c0:c0 + ff_chunk, :])
        f = part if f is None else f + part
    y_ref[...] = x1 + gate2_ref[...] * _rms(f, gqf_ref[...])


def _ff_chunk(d_ff):
    for c in (512, 256, 128):
        if d_ff % c == 0:
            return c
    return d_ff


def _tail_prompt(a, o, x, mods, g_post_mix, g_pre_ffn, g_post_ffn, woa, woo, wgu, wdn, layer, tm=512):
    b, s, d = x.shape
    row = lambda width: pl.BlockSpec((None, tm, width), lambda bi, i: (bi, i, 0))
    mod = lambda j: pl.BlockSpec((None, None, 1, d), lambda bi, i: (layer, bi, 0, j))
    stacked = [g_post_mix, g_pre_ffn, g_post_ffn, woa, woo, wgu, wdn]
    return pl.pallas_call(
        functools.partial(_tail_kernel, ff_chunk=_ff_chunk(wdn.shape[1])),
        grid=(b, s // tm),
        in_specs=[row(a.shape[2]), row(o.shape[2]), row(d), mod(2), mod(3), mod(4), mod(5)] +
                 [_layer_spec(t, layer) for t in stacked],
        out_specs=row(d),
        out_shape=jax.ShapeDtypeStruct((b, s, d), F32),
        compiler_params=_params(("arbitrary", "arbitrary")),
        name="tail_prompt",
    )(a, o, x, mods, mods, mods, mods, *stacked)


def _tail_sample(a, o, x, mods, g_post_mix, g_pre_ffn, g_post_ffn, woa, woo, wgu, wdn, layer):
    n, d = x.shape
    tm = mods.shape[1]
    row = lambda width: pl.BlockSpec((tm, width), lambda i: (i, 0))
    mod = lambda j: pl.BlockSpec((None, tm, d), lambda i: (layer, 0, j))
    stacked = [g_post_mix, g_pre_ffn, g_post_ffn, woa, woo, wgu, wdn]
    return pl.pallas_call(
        functools.partial(_tail_kernel, ff_chunk=_ff_chunk(wdn.shape[1])),
        grid=(n // tm,),
        in_specs=[row(a.shape[1]), row(o.shape[1]), row(d), mod(2), mod(3), mod(4), mod(5)] +
                 [_layer_spec(t, layer) for t in stacked],
        out_specs=row(d),
        out_shape=jax.ShapeDtypeStruct((n, d), F32),
        compiler_params=_params(("arbitrary",)),
        name="tail_sample",
    )(a, o, x, mods, mods, mods, mods, *stacked)


def kernel(x_prompt, x_sample, c_prompt, c_sample, cache_k, cache_v, cache_logf, page_table, w_ada, b_ada, g_pre_mix, g_post_mix, g_pre_ffn, g_post_ffn, w_in, b_f, g_sgu, w_s, b_s, w_out, w_gu, w_down):
    bsz, seq, d = x_prompt.shape
    db, t_new, _ = x_sample.shape
    depth = w_ada.shape[0]
    _, _, page, hb, hd = cache_k.shape
    ha = w_s.shape[1]
    wa, wb = ha * hd, hb * hd
    assert hd == HEAD_DIM and page == CHUNK and w_s.shape[2] == CHUNK and 2 * hd == LANES
    npp = seq // page

    mods = _ada(jnp.concatenate([c_prompt, c_sample], axis=0), w_ada, b_ada)
    mods_p = mods[:, :bsz].reshape(depth, bsz, 1, 6 * d)
    mods_s = mods[:, bsz:]

    o_main = 2 * wa + 3 * wb
    o_k = 2 * wa + wb
    w_main = w_in[:, :, :o_main].astype(BF16)
    w_uvqv = jnp.concatenate([w_main[:, :, :o_k], w_main[:, :, o_k + wb:]], axis=2)
    w_kv_t = jnp.swapaxes(w_main[:, :, o_k:], 1, 2)
    wf_t = jnp.swapaxes(w_in[:, :, o_main:], 1, 2)
    wf_t = jnp.pad(wf_t, ((0, 0), (0, BF16_ROWS - hb), (0, 0))).astype(BF16)
    woa = w_out[:, :wa].astype(BF16)
    woo = w_out[:, wa:].astype(BF16)
    wgu = w_gu.astype(BF16)
    wdn = w_down.astype(BF16)
    vec = lambda g: g.reshape(depth, 1, -1)
    g_pre_mix, g_post_mix, g_pre_ffn, g_post_ffn, g_sgu = map(
        vec, (g_pre_mix, g_post_mix, g_pre_ffn, g_post_ffn, g_sgu))
    b_f = b_f.reshape(depth, hb, 1)
    grp = jnp.arange(wa) // hd
    ones_bd = (grp[:, None] == grp[None, :]).astype(BF16)
    bias_exp = jnp.repeat(jnp.swapaxes(b_s, 1, 2), hd, axis=2)
    wvec = jnp.repeat(jnp.transpose(w_s[:, :, :t_new, :t_new], (0, 2, 3, 1)), hd, axis=3)
    wvec = wvec.reshape(depth, t_new * t_new, wa)
    bvec = bias_exp[:, :t_new]
    ck_t = jnp.transpose(cache_k, (0, 1, 3, 4, 2))
    cv_t = jnp.transpose(cache_v, (0, 1, 3, 4, 2))
    cf_t = jnp.transpose(cache_logf, (0, 1, 3, 2))

    xp = x_prompt
    xs = jnp.swapaxes(x_sample, 0, 1).reshape(t_new * db, d)
    k_all = jnp.zeros((bsz, npp, depth, hb, hd, page), F32)
    v_all = jnp.zeros((bsz, npp, depth, hb, hd, page), F32)
    pf, sk, sv, sf, sa = [], [], [], [], []
    per_sample = lambda t: jnp.swapaxes(t.reshape(t_new, db, -1), 0, 1)
    pad_rows = lambda t: jnp.pad(t, ((0, 0), (0, BF16_ROWS - t_new), (0, 0)))
    for l in range(depth):
        a, q, v_nat, k_all, v_all, lf, fcum = _mix_in_prompt(
            xp, mods_p, g_pre_mix, w_uvqv, w_kv_t, wf_t, b_f, g_sgu, ones_bd, w_s, bias_exp,
            k_all, v_all, l)
        pf.append(lf)
        u, va, qs, k, v, lf = _mix_in_sample(xs, mods_s, g_pre_mix, w_main, wf_t, b_f, g_sgu,
                                             ones_bd, l)
        a_s = _sgu_sample(u.reshape(t_new, db, wa), va.reshape(t_new, db, wa), wvec[l], bvec[l])
        k3, v3 = per_sample(k), per_sample(v)
        lf3 = jnp.transpose(lf.reshape(hb, t_new, db), (2, 0, 1))
        lf_pad = jnp.pad(lf3, ((0, 0), (0, 0), (0, LANES - t_new)))

        o, o_s = _fox(page_table, q, k_all, v_nat, fcum, per_sample(qs), pad_rows(k3), pad_rows(v3),
                      lf_pad, ck_t, cv_t, cf_t, l)

        xp = _tail_prompt(a, o, xp, mods_p, g_post_mix, g_pre_ffn, g_post_ffn, woa, woo, wgu, wdn, l)
        xs = _tail_sample(a_s.reshape(t_new * db, wa),
                          jnp.swapaxes(o_s, 0, 1).reshape(t_new * db, wb),
                          xs, mods_s, g_post_mix, g_pre_ffn, g_post_ffn, woa, woo, wgu, wdn, l)
        sk.append(k3.reshape(db, t_new, hb, hd))
        sv.append(v3.reshape(db, t_new, hb, hd))
        sf.append(jnp.swapaxes(lf3, 1, 2))
        sa.append(per_sample(va).reshape(db, t_new, ha, hd))

    prompt_k = jnp.transpose(k_all, (0, 1, 2, 5, 3, 4))
    prompt_v = jnp.transpose(v_all, (0, 1, 2, 5, 3, 4))
    prompt_logf = jnp.stack(
        [jnp.swapaxes(t, 1, 2).reshape(bsz, npp, page, hb) for t in pf], axis=2)
    y_sample = jnp.swapaxes(xs.reshape(t_new, db, d), 0, 1)
    return (xp, y_sample, prompt_k, prompt_v, prompt_logf, jnp.stack(sk, axis=1),
            jnp.stack(sv, axis=1), jnp.stack(sf, axis=1), jnp.stack(sa, axis=1))
```

```python
import functools

import jax
import jax.numpy as jnp
from jax import lax
from jax.experimental import pallas as pl
from jax.experimental.pallas import tpu as pltpu

F32 = jnp.float32
BF16 = jnp.bfloat16

HEAD_DIM = 64
CHUNK = 128
EPS = 1e-6
LANES = 128
SUBLANES = 8
BF16_ROWS = 16
NEG = -1e30
LOG2E = 1.4426950408889634
VMEM_LIMIT = 56 * 1024 * 1024

_NT = (((1,), (1,)), ((), ()))


def _params(sem, vmem=VMEM_LIMIT):
    return pltpu.CompilerParams(dimension_semantics=sem, vmem_limit_bytes=vmem)


def _rms(x, g):
    return x * lax.rsqrt(jnp.mean(x * x, axis=-1, keepdims=True) + EPS) * g


def _dot(a, b):
    return jnp.dot(a, b, preferred_element_type=F32)


def _dot_nt(a, b):
    return lax.dot_general(a, b, _NT, preferred_element_type=F32)


def _mod(x, n):
    return jnp.bitwise_and(x, n - 1) if n & (n - 1) == 0 else x % n


def _div(x, n):
    return lax.shift_right_logical(x, n.bit_length() - 1) if n & (n - 1) == 0 else x // n


def _split3(x):
    x1 = x.astype(BF16)
    r1 = x - x1.astype(F32)
    x2 = r1.astype(BF16)
    x3 = (r1 - x2.astype(F32)).astype(BF16)
    return x1, x2, x3


def _log_sigmoid(x):
    return jnp.minimum(x, 0.0) - jnp.log1p(jnp.exp(-jnp.abs(x)))


def _layer_spec(arr, layer):
    nd = arr.ndim - 1
    return pl.BlockSpec((None,) + arr.shape[1:], lambda *_: (layer,) + (0,) * nd,
                        pipeline_mode=pl.Buffered(1))


def _const_spec(shape):
    nd = len(shape)
    return pl.BlockSpec(shape, lambda *_: (0,) * nd, pipeline_mode=pl.Buffered(1))


def _ada_kernel(c_ref, w_ref, b_ref, o_ref):
    c = c_ref[...]
    s = (c * jax.nn.sigmoid(c)).astype(BF16)
    o_ref[...] = _dot(s, w_ref[...].astype(BF16)) + b_ref[...]


def _ada(c_all, w_ada, b_ada, tn=1024):
    depth, d, e = w_ada.shape
    n = c_all.shape[0]
    return pl.pallas_call(
        _ada_kernel,
        grid=(depth, e // tn),
        in_specs=[
            pl.BlockSpec((n, d), lambda l, j: (0, 0)),
            pl.BlockSpec((None, d, tn), lambda l, j: (l, 0, j)),
            pl.BlockSpec((None, 1, tn), lambda l, j: (l, 0, j)),
        ],
        out_specs=pl.BlockSpec((None, n, tn), lambda l, j: (l, 0, j)),
        out_shape=jax.ShapeDtypeStruct((depth, n, e), F32),
        compiler_params=_params(("arbitrary", "arbitrary")),
        name="ada",
    )(c_all, w_ada, b_ada.reshape(depth, 1, e))


def _norm_mod(x_ref, shift_ref, scale_ref, g_ref):
    h = _rms(x_ref[...], g_ref[...]) * (1.0 + scale_ref[...]) + shift_ref[...]
    return h.astype(BF16)


def _gates(z, wa, gsgu_ref, ones_ref):
    u = jax.nn.gelu(z[:, :wa])
    va = jax.nn.gelu(z[:, wa:2 * wa])
    gs = _dot((va * va).astype(BF16), ones_ref[...])
    va = va * lax.rsqrt(gs * (1.0 / HEAD_DIM) + EPS) * gsgu_ref[...]
    return u, va


def _logf_t(wf_ref, bf_ref, hb):
    nh = bf_ref.shape[0]
    return _log_sigmoid(_dot_nt(wf_ref[...], hb)[:nh] + bf_ref[...])


def _mix_in_prompt_kernel(x_ref, shift_ref, scale_ref, g_ref, w_ref, wkv_ref, wf_ref, bf_ref,
                          gsgu_ref, ones_ref, ws_ref, bias_ref, kin_ref, vin_ref,
                          a_ref, q_ref, vn_ref, k_ref, v_ref, lf_ref, fc_ref, carry_ref):
    del kin_ref, vin_ref
    tm = x_ref.shape[0]
    wa = ones_ref.shape[0]
    hb = _norm_mod(x_ref, shift_ref, scale_ref, g_ref)
    z = _dot(hb, w_ref[...])
    u, va = _gates(z, wa, gsgu_ref, ones_ref)
    wb = (z.shape[1] - 2 * wa) // 2
    q_ref[...] = (z[:, 2 * wa:2 * wa + wb] * (HEAD_DIM ** -0.5 * LOG2E)).astype(BF16)
    vn_ref[...] = z[:, 2 * wa + wb:].astype(BF16)

    kv_t = _dot_nt(wkv_ref[...], hb)
    for pg in range(tm // CHUNK):
        cols = slice(pg * CHUNK, (pg + 1) * CHUNK)
        k_ref[pg] = kv_t[:wb, cols].reshape(k_ref.shape[1:])
        v_ref[pg] = kv_t[wb:, cols].reshape(v_ref.shape[1:])

    logf_t = _logf_t(wf_ref, bf_ref, hb)
    lf_ref[...] = logf_t

    @pl.when(pl.program_id(1) == 0)
    def _():
        carry_ref[...] = jnp.zeros_like(carry_ref)

    r = lax.broadcasted_iota(jnp.int32, (tm, tm), 0)
    c = lax.broadcasted_iota(jnp.int32, (tm, tm), 1)
    tri = jnp.where(r <= c, 1.0, 0.0).astype(BF16)
    l1, l2, l3 = _split3(logf_t)
    fc = _dot(l1, tri) + _dot(l2, tri) + _dot(l3, tri) + carry_ref[:, :1]
    fc_ref[...] = fc
    carry_ref[...] = jnp.broadcast_to(fc[:, tm - 1:tm], carry_ref.shape)

    n_heads = ws_ref.shape[0]
    rr = lax.broadcasted_iota(jnp.int32, (CHUNK, CHUNK), 0)
    cc = lax.broadcasted_iota(jnp.int32, (CHUNK, CHUNK), 1)
    low = cc <= rr
    first = lax.broadcasted_iota(jnp.int32, (CHUNK, LANES), 1) < HEAD_DIM
    wts = [jnp.where(low, ws_ref[hh], 0.0).astype(BF16) for hh in range(n_heads)]
    for ci in range(tm // CHUNK):
        rows = slice(ci * CHUNK, (ci + 1) * CHUNK)
        for p in range(n_heads // 2):
            cols = slice(p * LANES, (p + 1) * LANES)
            vp = va[rows, cols].astype(BF16)
            mixed = jnp.where(first, _dot(wts[2 * p], vp), _dot(wts[2 * p + 1], vp))
            a_ref[rows, cols] = (u[rows, cols] * (mixed + bias_ref[:, cols])).astype(BF16)


def _mix_in_prompt(x, mods, g_pre, w_uvqv, w_kv_t, wf_t, b_f, g_sgu, ones_bd, w_s, bias_exp,
                   k_all, v_all, layer, tm=512):
    b, s, d = x.shape
    wa = ones_bd.shape[0]
    wb = (w_uvqv.shape[2] - 2 * wa) // 2
    nh = b_f.shape[1]
    ppt = tm // CHUNK
    row = lambda width: pl.BlockSpec((None, tm, width), lambda bi, i: (bi, i, 0))
    mod = lambda j: pl.BlockSpec((None, None, 1, d), lambda bi, i: (layer, bi, 0, j))
    lane_major = pl.BlockSpec((None, nh, tm), lambda bi, i: (bi, 0, i))
    kv_out = pl.BlockSpec((None, ppt, None) + k_all.shape[3:],
                          lambda bi, i: (bi, i, layer, 0, 0, 0))
    any_spec = pl.BlockSpec(memory_space=pl.ANY)
    stacked = [g_pre, w_uvqv, w_kv_t, wf_t, b_f, g_sgu]
    ins = [x, mods, mods] + stacked + [ones_bd, w_s, bias_exp, k_all, v_all]
    return pl.pallas_call(
        _mix_in_prompt_kernel,
        grid=(b, s // tm),
        in_specs=[row(d), mod(0), mod(1)] + [_layer_spec(t, layer) for t in stacked] +
                 [_const_spec(ones_bd.shape), _layer_spec(w_s, layer), _layer_spec(bias_exp, layer),
                  any_spec, any_spec],
        out_specs=[row(wa), row(wb), row(wb), kv_out, kv_out, lane_major, lane_major],
        out_shape=[jax.ShapeDtypeStruct((b, s, wa), BF16), jax.ShapeDtypeStruct((b, s, wb), BF16),
                   jax.ShapeDtypeStruct((b, s, wb), BF16),
                   jax.ShapeDtypeStruct(k_all.shape, F32), jax.ShapeDtypeStruct(v_all.shape, F32),
                   jax.ShapeDtypeStruct((b, nh, s), F32), jax.ShapeDtypeStruct((b, nh, s), F32)],
        scratch_shapes=[pltpu.VMEM((nh, LANES), F32)],
        input_output_aliases={len(ins) - 2: 3, len(ins) - 1: 4},
        compiler_params=_params(("arbitrary", "arbitrary")),
        name="mix_in_prompt",
    )(*ins)


def _mix_in_sample_kernel(x_ref, shift_ref, scale_ref, g_ref, w_ref, wf_ref, bf_ref, gsgu_ref,
                          ones_ref, u_ref, va_ref, q_ref, k_ref, v_ref, lf_ref):
    wa = ones_ref.shape[0]
    hb = _norm_mod(x_ref, shift_ref, scale_ref, g_ref)
    z = _dot(hb, w_ref[...])
    u, va = _gates(z, wa, gsgu_ref, ones_ref)
    u_ref[...] = u
    va_ref[...] = va
    o = 2 * wa
    wb = (z.shape[1] - o) // 3
    q_ref[...] = z[:, o:o + wb] * (HEAD_DIM ** -0.5)
    k_ref[...] = z[:, o + wb:o + 2 * wb]
    v_ref[...] = z[:, o + 2 * wb:o + 3 * wb]
    lf_ref[...] = _logf_t(wf_ref, bf_ref, hb)


def _mix_in_sample(x, mods, g_pre, w_main, wf_t, b_f, g_sgu, ones_bd, layer):
    n, d = x.shape
    tm = mods.shape[1]
    wa = ones_bd.shape[0]
    wb = (w_main.shape[2] - 2 * wa) // 3
    nh = b_f.shape[1]
    row = lambda width: pl.BlockSpec((tm, width), lambda i: (i, 0))
    mod = lambda j: pl.BlockSpec((None, tm, d), lambda i: (layer, 0, j))
    stacked = [g_pre, w_main, wf_t, b_f, g_sgu]
    return pl.pallas_call(
        _mix_in_sample_kernel,
        grid=(n // tm,),
        in_specs=[row(d), mod(0), mod(1)] + [_layer_spec(t, layer) for t in stacked] +
                 [_const_spec(ones_bd.shape)],
        out_specs=[row(wa), row(wa), row(wb), row(wb), row(wb),
                   pl.BlockSpec((nh, tm), lambda i: (0, i))],
        out_shape=[jax.ShapeDtypeStruct((n, wa), F32), jax.ShapeDtypeStruct((n, wa), F32),
                   jax.ShapeDtypeStruct((n, wb), F32), jax.ShapeDtypeStruct((n, wb), F32),
                   jax.ShapeDtypeStruct((n, wb), F32), jax.ShapeDtypeStruct((nh, n), F32)],
        compiler_params=_params(("arbitrary",)),
        name="mix_in_sample",
    )(x, mods, mods, *stacked, ones_bd)


def _sgu_sample_kernel(u_ref, va_ref, w_ref, b_ref, a_ref):
    t_new = u_ref.shape[0]
    for t in range(t_new):
        mixed = b_ref[t:t + 1, :]
        for j in range(t + 1):
            mixed = mixed + w_ref[t * t_new + j:t * t_new + j + 1, :] * va_ref[j]
        a_ref[t] = (u_ref[t] * mixed).astype(BF16)


def _sgu_sample(u, va, wvec, bvec):
    return pl.pallas_call(
        _sgu_sample_kernel,
        out_shape=jax.ShapeDtypeStruct(u.shape, BF16),
        name="sgu_sample",
    )(u, va, wvec, bvec)


def _pieces3(x):
    x1 = x.astype(BF16).astype(F32)
    r1 = x - x1
    x2 = r1.astype(BF16).astype(F32)
    x3 = (r1 - x2).astype(BF16).astype(F32)
    return x1, x2, x3


def _flash_key_block(k_ref, v_ref, f_ref, j, tq):
    ppb = tq // CHUNK
    lane = lax.broadcasted_iota(jnp.int32, (tq, LANES), 1)
    row8 = lax.broadcasted_iota(jnp.int32, (SUBLANES, tq), 0)
    pad = jnp.zeros((HEAD_DIM - SUBLANES, tq), F32)
    k_t = jnp.concatenate([k_ref[j * ppb + p].reshape(LANES, CHUNK) for p in range(ppb)], axis=1)
    v = v_ref[j * tq:(j + 1) * tq, :].astype(F32)
    ka, va = [], []
    for hh in range(2):
        x1, x2, x3 = _pieces3(f_ref[hh, :, j * tq:(j + 1) * tq] * LOG2E)
        fp = jnp.where(row8 == 0, x1, jnp.where(row8 == 1, x2, jnp.where(row8 == 2, x3, 0.0)))
        if hh == 0:
            ka.append(jnp.concatenate([k_t[:HEAD_DIM], fp, pad], axis=0).astype(BF16))
            va.append(jnp.where(lane == HEAD_DIM, 1.0, v).astype(BF16))
        else:
            ka.append(jnp.concatenate([fp, pad, k_t[HEAD_DIM:]], axis=0).astype(BF16))
            va.append(jnp.where(lane == 0, 1.0, v).astype(BF16))
    return ka, va


def _flash_head(q_ref, k_aug, v_aug, i, hh, tq):
    lane = lax.broadcasted_iota(jnp.int32, (tq, LANES), 1)
    first = lane < HEAD_DIM
    causal = (lax.broadcasted_iota(jnp.int32, (tq, tq), 1) <=
              lax.broadcasted_iota(jnp.int32, (tq, tq), 0))
    q = q_ref[i * tq:(i + 1) * tq, :].astype(F32)
    if hh == 0:
        qh = jnp.where(first, q, jnp.where(lane < HEAD_DIM + 3, -1.0, 0.0)).astype(BF16)
    else:
        qh = jnp.where(first, jnp.where(lane < 3, -1.0, 0.0), q).astype(BF16)
    m = jnp.full((tq, 1), NEG, F32)
    acc = jnp.zeros((tq, LANES), F32)
    for j in range(i + 1):
        s = _dot(qh, k_aug[j][hh])
        if j == i:
            s = jnp.where(causal, s, NEG)
        m_new = jnp.maximum(m, jnp.max(s, axis=1, keepdims=True))
        p = jnp.exp2(s - m_new).astype(BF16)
        acc = jnp.exp2(m - m_new) * acc + _dot(p, v_aug[j][hh])
        m = m_new
    ones_lane = HEAD_DIM if hh == 0 else 0
    return acc / acc[:, ones_lane:ones_lane + 1]


def _paged_sample(q, kn, vn, lf, f, k_page, v_page, n_pages, n_heads):
    t_new, wb = q.shape
    hd = wb // n_heads
    page = f.shape[1]
    rows = t_new * n_heads

    after = jnp.where(lax.broadcasted_iota(jnp.int32, (page, page), 0) >
                      lax.broadcasted_iota(jnp.int32, (page, page), 1), 1.0, 0.0).astype(BF16)
    ones = jnp.ones((page, page), BF16)
    f1, f2, f3 = _split3(f)
    tot = _dot(f1, ones) + _dot(f2, ones) + _dot(f3, ones)
    nr = f.shape[0]
    rr = lax.broadcasted_iota(jnp.int32, (nr, nr), 0)
    cc = lax.broadcasted_iota(jnp.int32, (nr, nr), 1)
    later = (_mod(rr, n_heads) == _mod(cc, n_heads)) & (cc > rr)
    su = jnp.where(later, 1.0, 0.0).astype(BF16)
    t1, t2, t3 = _split3(tot)
    r_all = (_dot(f1, after) + _dot(f2, after) + _dot(f3, after) +
             _dot(su, t1) + _dot(su, t2) + _dot(su, t3))

    own = (lax.broadcasted_iota(jnp.int32, (n_heads, wb), 0) ==
           _div(lax.broadcasted_iota(jnp.int32, (n_heads, wb), 1), hd))
    qbd = jnp.concatenate(
        [jnp.where(own, jnp.broadcast_to(q[t:t + 1, :], (n_heads, wb)), 0.0) for t in range(t_new)],
        axis=0).astype(BF16)

    s_past = []
    for p in range(n_pages):
        r8 = r_all[p * n_heads:(p + 1) * n_heads, :]
        s_past.append(_dot(qbd, k_page(p)) + jnp.concatenate([r8] * t_new, axis=0))

    g = lf
    ln = lax.broadcasted_iota(jnp.int32, g.shape, 1)
    sh = 1
    while sh < t_new:
        g = g + jnp.where(ln >= sh, pltpu.roll(g, sh, axis=1), 0.0)
        sh *= 2
    n_pad = kn.shape[0]
    g = jnp.concatenate([g[:, :n_pad]] * t_new, axis=0)
    s_new = _dot_nt(qbd, kn.astype(BF16)) - g
    rn = lax.broadcasted_iota(jnp.int32, (rows, n_pad), 0)
    cn = lax.broadcasted_iota(jnp.int32, (rows, n_pad), 1)
    s_new = jnp.where(cn <= _div(rn, n_heads), s_new, NEG)

    m = jnp.max(s_new, axis=1, keepdims=True)
    m_el = s_past[0]
    for s in s_past[1:]:
        m_el = jnp.maximum(m_el, s)
    m = jnp.maximum(m, jnp.max(m_el, axis=1, keepdims=True))
    pr = jnp.exp(s_new - m)
    l = jnp.sum(pr, axis=1, keepdims=True)
    acc = _dot(pr.astype(BF16), vn.astype(BF16))
    l_el = None
    for p in range(n_pages):
        pr = jnp.exp(s_past[p] - m)
        l_el = pr if l_el is None else l_el + pr
        acc = acc + _dot_nt(pr.astype(BF16), v_page(p))
    l = l + jnp.sum(l_el, axis=1, keepdims=True)

    own_rows = jnp.concatenate([own] * t_new, axis=0)
    o = jnp.where(own_rows, acc / l, 0.0)
    return jnp.sum(o.reshape(t_new, n_heads, wb), axis=1)


def _fox_kernel(pt_ref, q_ref, k_ref, v_ref, f_ref, qs_ref, kn_ref, vn_ref, lf_ref,
                ck_ref, cv_ref, cf_ref, o_ref, os_ref, kbuf, vbuf, fbuf, ksem, vsem, fsem,
                *, layer, tq):
    slots = kbuf.shape[0]
    ahead = slots - 1
    n_pages = pt_ref.shape[1]
    n_heads, hd, page = kbuf.shape[2:]
    wb = n_heads * hd
    spp = qs_ref.shape[0]
    n_q = q_ref.shape[0] // tq
    step = pl.program_id(0) * pl.num_programs(1) + pl.program_id(1)
    n_samples = pl.num_programs(0) * pl.num_programs(1) * spp

    def copies(n, sl):
        out = []
        for p in range(n_pages):
            pg = pt_ref[n, p]
            out.append(pltpu.make_async_copy(ck_ref.at[pg, layer], kbuf.at[sl, p], ksem.at[sl]))
            out.append(pltpu.make_async_copy(cv_ref.at[pg, layer], vbuf.at[sl, p], vsem.at[sl]))
            out.append(pltpu.make_async_copy(cf_ref.at[pg, layer], fbuf.at[sl, p], fsem.at[sl]))
        return out

    @pl.when(step == 0)
    def _():
        for n0 in range(ahead):
            for cp in copies(n0, n0):
                cp.start()

    plan, load = [[] for _ in range(spp)], [0] * spp
    for i in reversed(range(n_q)):
        for hh in range(2):
            seg = load.index(min(load))
            plan[seg].append((i, hh))
            load[seg] += i + 1
    first = lax.broadcasted_iota(jnp.int32, (tq, LANES), 1) < HEAD_DIM
    k_aug, v_aug, done = [], [], {}
    for s in range(spp):
        n = step * spp + s
        sl = lax.rem(n, slots)

        @pl.when(n + ahead < n_samples)
        def _(n=n):
            for cp in copies(n + ahead, lax.rem(n + ahead, slots)):
                cp.start()

        for cp in copies(n, sl):
            cp.wait()
        os_ref[s] = _paged_sample(
            qs_ref[s], kn_ref[s], vn_ref[s], lf_ref[s], fbuf[sl].reshape(n_pages * n_heads, page),
            lambda p, sl=sl: kbuf[sl, p].reshape(wb, page).astype(BF16),
            lambda p, sl=sl: vbuf[sl, p].reshape(wb, page).astype(BF16), n_pages, n_heads)
        for i, hh in plan[s]:
            while len(k_aug) <= i:
                ka, va = _flash_key_block(k_ref, v_ref, f_ref, len(k_aug), tq)
                k_aug.append(ka)
                v_aug.append(va)
            done[i, hh] = _flash_head(q_ref, k_aug, v_aug, i, hh, tq)
            if (i, 1 - hh) in done:
                o_ref[i * tq:(i + 1) * tq, :] = jnp.where(
                    first, done[i, 0], done[i, 1]).astype(BF16)


def _fox(page_table, q, k_all, v_nat, fcum, qs, kn, vn, lf, cache_k, cache_v, cache_f, layer,
         tq=512):
    b, s, wb = q.shape
    nh = fcum.shape[1]
    npp, hd, page = k_all.shape[1], k_all.shape[4], k_all.shape[5]
    db, t_new, _ = qs.shape
    n_pages = page_table.shape[1]
    n_pairs = wb // LANES
    spp = db // (b * n_pairs)
    slots = 3
    assert spp * b * n_pairs == db and db >= slots and s % tq == 0
    f4 = fcum.reshape(b, nh, 1, s)
    full = pl.BlockSpec((None, s, LANES), lambda bi, hp, pt: (bi, 0, hp))
    pair = pl.BlockSpec((None, npp, None, 2, hd, page),
                        lambda bi, hp, pt: (bi, 0, layer, hp, 0, 0))
    per = lambda t: pl.BlockSpec((spp,) + t.shape[1:],
                                 lambda bi, hp, pt: (bi * n_pairs + hp, 0, 0))
    any_spec = pl.BlockSpec(memory_space=pl.ANY)
    grid_spec = pltpu.PrefetchScalarGridSpec(
        num_scalar_prefetch=1,
        grid=(b, n_pairs),
        in_specs=[full, pair, full,
                  pl.BlockSpec((None, 2, 1, s), lambda bi, hp, pt: (bi, hp, 0, 0)),
                  per(qs), per(kn), per(vn), per(lf), any_spec, any_spec, any_spec],
        out_specs=[full, per(qs)],
        scratch_shapes=[
            pltpu.VMEM((slots, n_pages, nh, hd, page), F32),
            pltpu.VMEM((slots, n_pages, nh, hd, page), F32),
            pltpu.VMEM((slots, n_pages, nh, page), F32),
            pltpu.SemaphoreType.DMA((slots,)),
            pltpu.SemaphoreType.DMA((slots,)),
            pltpu.SemaphoreType.DMA((slots,)),
        ],
    )
    return pl.pallas_call(
        functools.partial(_fox_kernel, layer=layer, tq=tq),
        grid_spec=grid_spec,
        out_shape=[jax.ShapeDtypeStruct((b, s, wb), BF16),
                   jax.ShapeDtypeStruct((db, t_new, wb), F32)],
        compiler_params=_params(("arbitrary", "arbitrary")),
        name="fox",
    )(page_table, q, k_all, v_nat, f4, qs, kn, vn, lf, cache_k, cache_v, cache_f)


def _tail_kernel(a_ref, o_ref, x_ref, gate1_ref, shift_ref, scale_ref, gate2_ref,
                 gpm_ref, gpf_ref, gqf_ref, woa_ref, woo_ref, wgu_ref, wdn_ref, y_ref, *, ff_chunk):
    d_ff = wdn_ref.shape[0]
    y = _dot(a_ref[...].astype(BF16), woa_ref[...]) + _dot(o_ref[...].astype(BF16), woo_ref[...])
    x1 = x_ref[...] + gate1_ref[...] * _rms(y, gpm_ref[...])
    h = (_rms(x1, gpf_ref[...]) * (1.0 + scale_ref[...]) + shift_ref[...]).astype(BF16)
    f = None
    for c0 in range(0, d_ff, ff_chunk):
        gte = _dot(h, wgu_ref[:, c0:c0 + ff_chunk])
        up = _dot(h, wgu_ref[:, d_ff + c0:d_ff + c0 + ff_chunk])
        t = (gte * jax.nn.sigmoid(gte) * up).astype(BF16)
        part = _dot(t, wdn_ref[c0:c0 + ff_chunk, :])
        f = part if f is None else f + part
    y_ref[...] = x1 + gate2_ref[...] * _rms(f, gqf_ref[...])


def _ff_chunk(d_ff):
    for c in (512, 256, 128):
        if d_ff % c == 0:
            return c
    return d_ff


def _tail_prompt(a, o, x, mods, g_post_mix, g_pre_ffn, g_post_ffn, woa, woo, wgu, wdn, layer, tm=512):
    b, s, d = x.shape
    row = lambda width: pl.BlockSpec((None, tm, width), lambda bi, i: (bi, i, 0))
    mod = lambda j: pl.BlockSpec((None, None, 1, d), lambda bi, i: (layer, bi, 0, j))
    stacked = [g_post_mix, g_pre_ffn, g_post_ffn, woa, woo, wgu, wdn]
    return pl.pallas_call(
        functools.partial(_tail_kernel, ff_chunk=_ff_chunk(wdn.shape[1])),
        grid=(b, s // tm),
        in_specs=[row(a.shape[2]), row(o.shape[2]), row(d), mod(2), mod(3), mod(4), mod(5)] +
                 [_layer_spec(t, layer) for t in stacked],
        out_specs=row(d),
        out_shape=jax.ShapeDtypeStruct((b, s, d), F32),
        compiler_params=_params(("arbitrary", "arbitrary")),
        name="tail_prompt",
    )(a, o, x, mods, mods, mods, mods, *stacked)


def _tail_sample(a, o, x, mods, g_post_mix, g_pre_ffn, g_post_ffn, woa, woo, wgu, wdn, layer):
    n, d = x.shape
    tm = mods.shape[1]
    row = lambda width: pl.BlockSpec((tm, width), lambda i: (i, 0))
    mod = lambda j: pl.BlockSpec((None, tm, d), lambda i: (layer, 0, j))
    stacked = [g_post_mix, g_pre_ffn, g_post_ffn, woa, woo, wgu, wdn]
    return pl.pallas_call(
        functools.partial(_tail_kernel, ff_chunk=_ff_chunk(wdn.shape[1])),
        grid=(n // tm,),
        in_specs=[row(a.shape[1]), row(o.shape[1]), row(d), mod(2), mod(3), mod(4), mod(5)] +
                 [_layer_spec(t, layer) for t in stacked],
        out_specs=row(d),
        out_shape=jax.ShapeDtypeStruct((n, d), F32),
        compiler_params=_params(("arbitrary",)),
        name="tail_sample",
    )(a, o, x, mods, mods, mods, mods, *stacked)


def kernel(x_prompt, x_sample, c_prompt, c_sample, cache_k, cache_v, cache_logf, page_table, w_ada, b_ada, g_pre_mix, g_post_mix, g_pre_ffn, g_post_ffn, w_in, b_f, g_sgu, w_s, b_s, w_out, w_gu, w_down):
    bsz, seq, d = x_prompt.shape
    db, t_new, _ = x_sample.shape
    depth = w_ada.shape[0]
    _, _, page, hb, hd = cache_k.shape
    ha = w_s.shape[1]
    wa, wb = ha * hd, hb * hd
    assert hd == HEAD_DIM and page == CHUNK and w_s.shape[2] == CHUNK and 2 * hd == LANES
    npp = seq // page

    mods = _ada(jnp.concatenate([c_prompt, c_sample], axis=0), w_ada, b_ada)
    mods_p = mods[:, :bsz].reshape(depth, bsz, 1, 6 * d)
    mods_s = mods[:, bsz:]

    o_main = 2 * wa + 3 * wb
    o_k = 2 * wa + wb
    w_main = w_in[:, :, :o_main].astype(BF16)
    w_uvqv = jnp.concatenate([w_main[:, :, :o_k], w_main[:, :, o_k + wb:]], axis=2)
    w_kv_t = jnp.swapaxes(w_main[:, :, o_k:], 1, 2)
    wf_t = jnp.swapaxes(w_in[:, :, o_main:], 1, 2)
    wf_t = jnp.pad(wf_t, ((0, 0), (0, BF16_ROWS - hb), (0, 0))).astype(BF16)
    woa = w_out[:, :wa].astype(BF16)
    woo = w_out[:, wa:].astype(BF16)
    wgu = w_gu.astype(BF16)
    wdn = w_down.astype(BF16)
    vec = lambda g: g.reshape(depth, 1, -1)
    g_pre_mix, g_post_mix, g_pre_ffn, g_post_ffn, g_sgu = map(
        vec, (g_pre_mix, g_post_mix, g_pre_ffn, g_post_ffn, g_sgu))
    b_f = b_f.reshape(depth, hb, 1)
    grp = jnp.arange(wa) // hd
    ones_bd = (grp[:, None] == grp[None, :]).astype(BF16)
    bias_exp = jnp.repeat(jnp.swapaxes(b_s, 1, 2), hd, axis=2)
    wvec = jnp.repeat(jnp.transpose(w_s[:, :, :t_new, :t_new], (0, 2, 3, 1)), hd, axis=3)
    wvec = wvec.reshape(depth, t_new * t_new, wa)
    bvec = bias_exp[:, :t_new]
    ck_t = jnp.transpose(cache_k, (0, 1, 3, 4, 2))
    cv_t = jnp.transpose(cache_v, (0, 1, 3, 4, 2))
    cf_t = jnp.transpose(cache_logf, (0, 1, 3, 2))

    xp = x_prompt
    xs = jnp.swapaxes(x_sample, 0, 1).reshape(t_new * db, d)
    k_all = jnp.zeros((bsz, npp, depth, hb, hd, page), F32)
    v_all = jnp.zeros((bsz, npp, depth, hb, hd, page), F32)
    pf, sk, sv, sf, sa = [], [], [], [], []
    per_sample = lambda t: jnp.swapaxes(t.reshape(t_new, db, -1), 0, 1)
    pad_rows = lambda t: jnp.pad(t, ((0, 0), (0, BF16_ROWS - t_new), (0, 0)))
    for l in range(depth):
        a, q, v_nat, k_all, v_all, lf, fcum = _mix_in_prompt(
            xp, mods_p, g_pre_mix, w_uvqv, w_kv_t, wf_t, b_f, g_sgu, ones_bd, w_s, bias_exp,
            k_all, v_all, l)
        pf.append(lf)
        u, va, qs, k, v, lf = _mix_in_sample(xs, mods_s, g_pre_mix, w_main, wf_t, b_f, g_sgu,
                                             ones_bd, l)
        a_s = _sgu_sample(u.reshape(t_new, db, wa), va.reshape(t_new, db, wa), wvec[l], bvec[l])
        k3, v3 = per_sample(k), per_sample(v)
        lf3 = jnp.transpose(lf.reshape(hb, t_new, db), (2, 0, 1))
        lf_pad = jnp.pad(lf3, ((0, 0), (0, 0), (0, LANES - t_new)))

        o, o_s = _fox(page_table, q, k_all, v_nat, fcum, per_sample(qs), pad_rows(k3), pad_rows(v3),
                      lf_pad, ck_t, cv_t, cf_t, l)

        xp = _tail_prompt(a, o, xp, mods_p, g_post_mix, g_pre_ffn, g_post_ffn, woa, woo, wgu, wdn, l)
        xs = _tail_sample(a_s.reshape(t_new * db, wa),
                          jnp.swapaxes(o_s, 0, 1).reshape(t_new * db, wb),
                          xs, mods_s, g_post_mix, g_pre_ffn, g_post_ffn, woa, woo, wgu, wdn, l)
        sk.append(k3.reshape(db, t_new, hb, hd))
        sv.append(v3.reshape(db, t_new, hb, hd))
        sf.append(jnp.swapaxes(lf3, 1, 2))
        sa.append(per_sample(va).reshape(db, t_new, ha, hd))

    prompt_k = jnp.transpose(k_all, (0, 1, 2, 5, 3, 4))
    prompt_v = jnp.transpose(v_all, (0, 1, 2, 5, 3, 4))
    prompt_logf = jnp.stack(
        [jnp.swapaxes(t, 1, 2).reshape(bsz, npp, page, hb) for t in pf], axis=2)
    y_sample = jnp.swapaxes(xs.reshape(t_new, db, d), 0, 1)
    return (xp, y_sample, prompt_k, prompt_v, prompt_logf, jnp.stack(sk, axis=1),
            jnp.stack(sv, axis=1), jnp.stack(sf, axis=1), jnp.stack(sa, axis=1))
```
